```python
import jax, jax.numpy as jnp
from jax import lax
import numpy as np

D_MODEL = 1024
BATCH = 16
SEQ = 2048
DEPTH = 4

N_A_LAYERS = DEPTH // 2
N_B_LAYERS = DEPTH - N_A_LAYERS
CONV_WIDTH = 3
HEAD_DIM = 128
KV_HEADS = D_MODEL // HEAD_DIM
DILATION_GROUPS = ((128, 1), (512, 4), (2048, 16))
N_GROUPS = len(DILATION_GROUPS)
Q_HEADS = N_GROUPS * KV_HEADS
ATTN_WIDTH = KV_HEADS * HEAD_DIM
ROPE_DIMS = HEAD_DIM // 4
ROPE_THETA = 500000.0
PEER_HEADS = 8
PEER_N_KEYS = 128
PEER_N_EXPERTS = PEER_N_KEYS ** 2
PEER_KEY_DIM = 256
PEER_HALF = PEER_KEY_DIM // 2
PEER_TOPK = 16
PEER_SELECTED = PEER_HEADS * PEER_TOPK
PEER_TOKEN_BLOCK = 128
PLE_DIM = 256
NORM_EPS = 1e-6

kernel_name = "yoco_shortconv_dilated_attn_peer"


def rms_norm(x, gain):
    xf = x.astype(jnp.float32)
    y = xf * lax.rsqrt(jnp.mean(xf * xf, axis=-1, keepdims=True) + NORM_EPS)
    return (y * gain.astype(jnp.float32)).astype(x.dtype)


def rope_tables(positions):
    inv_freq = 1.0 / (ROPE_THETA ** (jnp.arange(0, ROPE_DIMS, 2, dtype=jnp.float32) / ROPE_DIMS))
    ang = positions.astype(jnp.float32)[..., None] * inv_freq
    return jnp.cos(ang), jnp.sin(ang)


def apply_partial_rope(x, cos, sin):
    half = ROPE_DIMS // 2
    xf = x.astype(jnp.float32)
    x1, x2, rest = xf[..., :half], xf[..., half:ROPE_DIMS], xf[..., ROPE_DIMS:]
    c, s = cos[:, :, None, :], sin[:, :, None, :]
    out = jnp.concatenate([x1 * c - x2 * s, x2 * c + x1 * s, rest], axis=-1)
    return out.astype(x.dtype)


def short_conv_mixer(h, w_in, conv_w, w_out):
    b_gate, c_gate, u = jnp.split(h @ w_in, 3, axis=-1)
    z = lax.conv_general_dilated(c_gate * u, conv_w[:, None, :], window_strides=(1,),
                                 padding=((CONV_WIDTH - 1, 0),),
                                 dimension_numbers=('NWC', 'WIO', 'NWC'),
                                 feature_group_count=D_MODEL)
    return (b_gate * z) @ w_out


def dilated_window_attention(q, k, v, window, dilation):
    bsz, seq, nh, dh = q.shape
    span = window // dilation
    blk = span
    unit = dilation * blk
    s_pad = -(-seq // unit) * unit
    nblk = s_pad // unit
    pad = ((0, 0), (0, s_pad - seq), (0, 0), (0, 0))

    def to_blocks(t):
        return jnp.pad(t, pad).reshape(bsz, nblk, blk, dilation, nh, dh)

    def with_prev(t):
        prev = jnp.concatenate([jnp.zeros_like(t[:, :1]), t[:, :-1]], axis=1)
        return jnp.concatenate([prev, t], axis=2)

    qb = to_blocks(q)
    kw, vw = with_prev(to_blocks(k)), with_prev(to_blocks(v))
    scores = jnp.einsum('bnqrhd,bnkrhd->bnrhqk', qb, kw,
                        preferred_element_type=jnp.float32) * (dh ** -0.5)
    qi = jnp.arange(blk)[:, None]
    ki = jnp.arange(2 * blk)[None, :]
    dist = qi + blk - ki
    n_idx = jnp.arange(nblk)[:, None, None]
    valid = (dist >= 0) & (dist <= span) & (n_idx * blk + ki[None] - blk >= 0)
    scores = jnp.where(valid[None, :, None, None], scores, -jnp.inf)
    m = jnp.max(scores, axis=-1, keepdims=True)
    e = jnp.exp(scores - m)
    den = jnp.sum(e, axis=-1, keepdims=True)
    o = jnp.einsum('bnrhqk,bnkrhd->bnqrhd', e / den, vw.astype(jnp.float32))
    lse = (m + jnp.log(den))[..., 0]
    o = o.reshape(bsz, s_pad, nh, dh)[:, :seq]
    lse = jnp.transpose(lse, (0, 1, 4, 2, 3)).reshape(bsz, s_pad, nh)[:, :seq]
    return o, lse


def dilated_attention_mixer(h, w_q, q_gain, w_o, k_shared, v_shared, cos, sin):
    bsz, seq, _ = h.shape
    q = (h @ w_q).reshape(bsz, seq, Q_HEADS, HEAD_DIM)
    q = apply_partial_rope(rms_norm(q, q_gain), cos, sin)
    q = q.reshape(bsz, seq, N_GROUPS, KV_HEADS, HEAD_DIM)
    outs, lses = [], []
    for g, (window, dilation) in enumerate(DILATION_GROUPS):
        o_g, lse_g = dilated_window_attention(q[:, :, g], k_shared, v_shared, window, dilation)
        outs.append(o_g)
        lses.append(lse_g)
    weights = jax.nn.softmax(jnp.stack(lses, axis=0), axis=0)
    o = jnp.sum(weights[..., None] * jnp.stack(outs, axis=0), axis=0)
    return o.reshape(bsz, seq, ATTN_WIDTH).astype(h.dtype) @ w_o


def shared_kv(x, kv_gain, w_kv, k_gain, cos, sin):
    bsz, seq, _ = x.shape
    k, v = jnp.split(rms_norm(x, kv_gain) @ w_kv, 2, axis=-1)
    k = k.reshape(bsz, seq, KV_HEADS, HEAD_DIM)
    v = v.reshape(bsz, seq, KV_HEADS, HEAD_DIM)
    k = apply_partial_rope(rms_norm(k, k_gain), cos, sin)
    return k, v


def peer_channel_mixer(h, w_q, sub_keys, u, v):
    bsz, seq, dm = h.shape
    n_tok = bsz * seq
    hf = h.reshape(n_tok, dm)
    q = (hf @ w_q).astype(jnp.float32).reshape(n_tok, PEER_HEADS, 2, PEER_HALF)
    sub = jnp.einsum('thpc,hpkc->thpk', q, sub_keys.astype(jnp.float32))
    top_s, top_i = lax.top_k(sub, PEER_TOPK)
    cand_s = (top_s[:, :, 0, :, None] + top_s[:, :, 1, None, :]).reshape(n_tok, PEER_HEADS, PEER_TOPK ** 2)
    cand_i = (top_i[:, :, 0, :, None] * PEER_N_KEYS + top_i[:, :, 1, None, :]).reshape(n_tok, PEER_HEADS, PEER_TOPK ** 2)
    best_s, best_pos = lax.top_k(cand_s, PEER_TOPK)
    expert_idx = jnp.take_along_axis(cand_i, best_pos, axis=-1)
    gates = jax.nn.softmax(best_s, axis=-1)
    n_blk = n_tok // PEER_TOKEN_BLOCK

    def token_block(args):
        xb, idx, g = args
        act = jnp.einsum('cd,ced->ce', xb, u[idx], preferred_element_type=jnp.float32)
        a = (g * jax.nn.gelu(act, approximate=False)).astype(xb.dtype)
        return jnp.einsum('ce,ced->cd', a, v[idx])

    y = lax.map(token_block, (hf.reshape(n_blk, PEER_TOKEN_BLOCK, dm),
                              expert_idx.reshape(n_blk, PEER_TOKEN_BLOCK, PEER_SELECTED),
                              gates.reshape(n_blk, PEER_TOKEN_BLOCK, PEER_SELECTED)))
    return y.reshape(bsz, seq, dm)


def setup_inputs(seed: int = 0) -> dict:
    key = jax.random.key(seed)
    ks = jax.random.split(key, 24)
    f32 = jnp.float32

    def w(k, shape, fan_in):
        return jax.random.normal(k, shape, f32) * (fan_in ** -0.5)

    def gain(k, shape):
        return 1.0 + 0.01 * jax.random.normal(k, shape, f32)

    offsets = jax.random.randint(ks[2], (BATCH, 1), 0, 4096, dtype=jnp.int32)
    positions = offsets + jnp.arange(SEQ, dtype=jnp.int32)[None, :]
    return {
        "x": jax.random.normal(ks[0], (BATCH, SEQ, D_MODEL), f32),
        "p": jax.random.normal(ks[1], (DEPTH, BATCH, SEQ, PLE_DIM), f32),
        "positions": positions,
        "a_norm": gain(ks[3], (N_A_LAYERS, D_MODEL)),
        "a_w_in": w(ks[4], (N_A_LAYERS, D_MODEL, 3 * D_MODEL), D_MODEL),
        "a_conv": w(ks[5], (N_A_LAYERS, CONV_WIDTH, D_MODEL), CONV_WIDTH),
        "a_w_out": w(ks[6], (N_A_LAYERS, D_MODEL, D_MODEL), D_MODEL),
        "kv_norm": gain(ks[7], (D_MODEL,)),
        "w_kv": w(ks[8], (D_MODEL, 2 * ATTN_WIDTH), D_MODEL),
        "k_norm": gain(ks[9], (HEAD_DIM,)),
        "b_norm": gain(ks[10], (N_B_LAYERS, D_MODEL)),
        "b_w_q": w(ks[11], (N_B_LAYERS, D_MODEL, Q_HEADS * HEAD_DIM), D_MODEL),
        "q_norm": gain(ks[12], (N_B_LAYERS, HEAD_DIM)),
        "b_w_o": w(ks[13], (N_B_LAYERS, ATTN_WIDTH, D_MODEL), ATTN_WIDTH),
        "ffn_norm": gain(ks[14], (DEPTH, D_MODEL)),
        "peer_w_q": w(ks[15], (DEPTH, D_MODEL, PEER_HEADS * PEER_KEY_DIM), D_MODEL),
        "peer_sub_keys": w(ks[16], (DEPTH, PEER_HEADS, 2, PEER_N_KEYS, PEER_HALF), PEER_HALF),
        "peer_u": w(ks[17], (DEPTH, PEER_N_EXPERTS, D_MODEL), D_MODEL),
        "peer_v": w(ks[18], (DEPTH, PEER_N_EXPERTS, D_MODEL), PEER_TOPK),
        "ple_norm": gain(ks[19], (DEPTH, D_MODEL)),
        "ple_w_gate": w(ks[20], (DEPTH, D_MODEL, D_MODEL), D_MODEL),
        "ple_w_proj": w(ks[21], (DEPTH, PLE_DIM, D_MODEL), PLE_DIM),
    }


def reference(x, p, positions, a_norm, a_w_in, a_conv, a_w_out, kv_norm, w_kv, k_norm,
              b_norm, b_w_q, q_norm, b_w_o, ffn_norm, peer_w_q, peer_sub_keys, peer_u,
              peer_v, ple_norm, ple_w_gate, ple_w_proj):
    cos, sin = rope_tables(positions)
    k_sh, v_sh = None, None
    for i in range(DEPTH):
        if i < N_A_LAYERS:
            x = x + short_conv_mixer(rms_norm(x, a_norm[i]), a_w_in[i], a_conv[i], a_w_out[i])
        else:
            j = i - N_A_LAYERS
            x = x + dilated_attention_mixer(rms_norm(x, b_norm[j]), b_w_q[j], q_norm[j], b_w_o[j],
                                            k_sh, v_sh, cos, sin)
        x = x + peer_channel_mixer(rms_norm(x, ffn_norm[i]), peer_w_q[i], peer_sub_keys[i],
                                   peer_u[i], peer_v[i])
        x = x + jax.nn.sigmoid(rms_norm(x, ple_norm[i]) @ ple_w_gate[i]) * (p[i] @ ple_w_proj[i])
        if i == N_A_LAYERS - 1:
            k_sh, v_sh = shared_kv(x, kv_norm, w_kv, k_norm, cos, sin)
    return x
```

```python
import functools

import jax
import jax.numpy as jnp
from jax import lax
from jax.experimental import pallas as pl
from jax.experimental.pallas import tpu as pltpu

F32 = jnp.float32
BF16 = jnp.bfloat16
U32 = jnp.uint32

LANES = 128
SUBLANES = 8
VMEM_LIMIT_BYTES = 56 * 1024 * 1024

NORM_EPS = 1e-6
HEAD_DIM = 128
KV_HEADS = 8
N_GROUPS = 3
ROPE_DIMS = HEAD_DIM // 4
ROPE_THETA = 500000.0
DILATION_GROUPS = ((128, 1), (512, 4), (2048, 16))
PEER_HEADS = 8
PEER_N_KEYS = 128
PEER_TOPK = 16
INV_SQRT2 = 0.7071067811865476
NEG_INF = float("-inf")


def _cparams(sem):
    return pltpu.CompilerParams(dimension_semantics=sem, vmem_limit_bytes=VMEM_LIMIT_BYTES)


def _rms(x, gain):
    ms = jnp.mean(x * x, axis=-1, keepdims=True)
    return x * lax.rsqrt(ms + NORM_EPS) * gain


def _dot(a, b):
    return jnp.dot(a, b, preferred_element_type=F32)


def _conv_mixer_kernel(x_ref, gain_ref, win_ref, conv_ref, wout_ref, o_ref, gbuf_ref):
    j = pl.program_id(1)
    tm, dm = x_ref.shape
    x = x_ref[...]
    h = _rms(x, gain_ref[...]).astype(BF16)
    bcu = _dot(h, win_ref[...])
    b_gate = bcu[:, :dm]
    g = bcu[:, dm:2 * dm] * bcu[:, 2 * dm:]

    @pl.when(j == 0)
    def _():
        gbuf_ref[0:SUBLANES, :] = jnp.zeros((SUBLANES, dm), F32)

    gbuf_ref[SUBLANES:SUBLANES + tm, :] = g
    g1 = gbuf_ref[SUBLANES - 1:SUBLANES - 1 + tm, :]
    g2 = gbuf_ref[SUBLANES - 2:SUBLANES - 2 + tm, :]
    w = conv_ref[...]
    z = w[0:1, :] * g2 + w[1:2, :] * g1 + w[2:3, :] * g
    gbuf_ref[0:SUBLANES, :] = g[tm - SUBLANES:, :]
    y = _dot((b_gate * z).astype(BF16), wout_ref[...])
    o_ref[...] = x + y


def conv_mixer(x, gain, w_in, conv_w, w_out, *, seq, tm=512):
    n_tok, dm = x.shape
    tiles_per_seq = seq // tm
    row = lambda b, j: (b * tiles_per_seq + j, 0)
    const = lambda b, j: (0, 0)
    return pl.pallas_call(
        _conv_mixer_kernel,
        grid=(n_tok // seq, tiles_per_seq),
        in_specs=[
            pl.BlockSpec((tm, dm), row),
            pl.BlockSpec((1, dm), const),
            pl.BlockSpec((dm, 3 * dm), const),
            pl.BlockSpec(conv_w.shape, const),
            pl.BlockSpec((dm, dm), const),
        ],
        out_specs=pl.BlockSpec((tm, dm), row),
        out_shape=jax.ShapeDtypeStruct(x.shape, x.dtype),
        scratch_shapes=[pltpu.VMEM((tm + SUBLANES, dm), F32)],
        input_output_aliases={0: 0},
        compiler_params=_cparams(("arbitrary", "arbitrary")),
        name="conv_mixer",
    )(x, gain, w_in, conv_w, w_out)


def _staircase():
    return [(k1, k2) for k1 in range(PEER_TOPK) for k2 in range(PEER_TOPK) if (k1 + 1) * (k2 + 1) <= PEER_TOPK]


def _tree_max(vals):
    vals = list(vals)
    while len(vals) > 1:
        nxt = [jnp.maximum(vals[a], vals[a + 1]) for a in range(0, len(vals) - 1, 2)]
        if len(vals) % 2:
            nxt.append(vals[-1])
        vals = nxt
    return vals[0]


def _dup_bf16_bits(v):
    bits = lax.bitcast_convert_type(v, U32)
    return bits | (bits >> 16)


def _peer_route_kernel(x_ref, gain_ref, wqt_ref, keys_ref,
                       ht_ref, kap_ref, alp_ref, rho_ref, bet_ref, s1_ref, s2_ref):
    tr = x_ref.shape[0]
    h = _rms(x_ref[...], gain_ref[...])
    ht = h.T.astype(BF16)
    ht_ref[...] = ht
    qt = _dot(wqt_ref[...], ht).astype(BF16)

    tops = {}
    for hd in range(PEER_HEADS):
        for half in range(2):
            r0 = (hd * 2 + half) * PEER_N_KEYS
            s = _dot(keys_ref[hd, half], qt[r0:r0 + PEER_N_KEYS, :])
            (s1_ref if half == 0 else s2_ref)[hd] = s
            work = s
            rank = jnp.full(s.shape, float(PEER_TOPK), F32)
            top = []
            for k in range(PEER_TOPK):
                m = jnp.max(work, axis=0, keepdims=True)
                eq = work == m
                if half == 1:
                    rank = jnp.where(eq, float(k), rank)
                work = jnp.where(eq, NEG_INF, work)
                top.append(m)
            tops[hd, half] = top
            if half == 1:
                rho_ref[hd] = rank.astype(BF16)

    a_k = [jnp.concatenate([tops[hd, 0][k] for hd in range(PEER_HEADS)], axis=0) for k in range(PEER_TOPK)]
    b_k = [jnp.concatenate([tops[hd, 1][k] for hd in range(PEER_HEADS)], axis=0) for k in range(PEER_TOPK)]
    cands = [a_k[k1] + b_k[k2] for k1, k2 in _staircase()]
    work = cands
    for _ in range(PEER_TOPK - 1):
        m = _tree_max(work)
        work = [jnp.where(c == m, NEG_INF, c) for c in work]
    tau = _tree_max(work)
    top_sum = cands[0]
    z = None
    for c in cands:
        e = jnp.where(c >= tau, jnp.exp(c - top_sum), 0.0)
        z = e if z is None else z + e
    inv_z = 1.0 / z

    for hd in range(PEER_HEADS):
        s1 = s1_ref[hd]
        s2 = s2_ref[hd]
        tau_h = tau[hd:hd + 1, :]
        kap = jnp.zeros(s1.shape, F32)
        for k2 in range(PEER_TOPK):
            kap = kap + jnp.where(s1 + tops[hd, 1][k2] >= tau_h, 1.0, 0.0)
        alp = jnp.exp(s1 - tops[hd, 0][0]).astype(BF16).astype(F32)
        bet = jnp.exp(s2 - tops[hd, 1][0]) * inv_z[hd:hd + 1, :]
        kap_ref[hd] = _dup_bf16_bits(kap)
        alp_ref[hd] = _dup_bf16_bits(alp)
        bet_ref[hd] = bet.astype(BF16)


def peer_route(x, gain, wq_t, sub_keys, *, tr=256):
    n_tok, dm = x.shape
    nq = wq_t.shape[0]
    route_spec = pl.BlockSpec((PEER_HEADS, PEER_N_KEYS, tr), lambda t: (0, 0, t))
    route_shape = (PEER_HEADS, PEER_N_KEYS, n_tok)
    return pl.pallas_call(
        _peer_route_kernel,
        grid=(n_tok // tr,),
        in_specs=[
            pl.BlockSpec((tr, dm), lambda t: (t, 0)),
            pl.BlockSpec((1, dm), lambda t: (0, 0)),
            pl.BlockSpec((nq, dm), lambda t: (0, 0)),
            pl.BlockSpec(sub_keys.shape, lambda t: (0, 0, 0, 0)),
        ],
        out_specs=[pl.BlockSpec((dm, tr), lambda t: (0, t)), route_spec, route_spec, route_spec, route_spec],
        out_shape=[
            jax.ShapeDtypeStruct((dm, n_tok), BF16),
            jax.ShapeDtypeStruct(route_shape, U32),
            jax.ShapeDtypeStruct(route_shape, U32),
            jax.ShapeDtypeStruct(route_shape, BF16),
            jax.ShapeDtypeStruct(route_shape, BF16),
        ],
        scratch_shapes=[pltpu.VMEM((PEER_HEADS, PEER_N_KEYS, tr), F32)] * 2,
        compiler_params=_cparams(("parallel",)),
        name="peer_route",
    )(x, gain, wq_t, sub_keys)


def _peer_dense_kernel(x_ref, ht_ref, u_ref, vt_ref, kap_ref, alp_ref, rho_in_ref, bet_in_ref,
                       o_ref, acc_ref, rho_ref, bet_ref, *, sub_experts):
    c = pl.program_id(1)
    te = u_ref.shape[0]
    tm = ht_ref.shape[1]
    n_sub = te // sub_experts
    blocks_per_sub = sub_experts // PEER_N_KEYS
    blocks_per_chunk = te // PEER_N_KEYS
    assert blocks_per_chunk == SUBLANES

    @pl.when(c == 0)
    def _():
        acc_ref[...] = jnp.zeros_like(acc_ref)
        rho_ref[...] = rho_in_ref[...]
        bet_ref[...] = bet_in_ref[...]

    ht = ht_ref[...]
    row0 = pl.multiple_of(c * SUBLANES, SUBLANES)
    for s in range(n_sub):
        rows = pl.ds(s * sub_experts, sub_experts)
        act = _dot(u_ref[rows, :], ht)
        act = 0.5 * act * (1.0 + lax.erf(act * INV_SQRT2))
        actb = act.astype(BF16)
        parts = []
        for b in range(blocks_per_sub):
            r = s * blocks_per_sub + b
            gate = None
            for hd in range(PEER_HEADS):
                kap8 = kap_ref[hd, pl.ds(row0, SUBLANES), :]
                alp8 = alp_ref[hd, pl.ds(row0, SUBLANES), :]
                kap = pltpu.bitcast(jnp.broadcast_to(kap8[r:r + 1, :], (PEER_N_KEYS // 2, tm)), BF16)
                alp = pltpu.bitcast(jnp.broadcast_to(alp8[r:r + 1, :], (PEER_N_KEYS // 2, tm)), BF16)
                term = jnp.where(rho_ref[hd] < kap, alp * bet_ref[hd], jnp.zeros((), BF16))
                gate = term if gate is None else gate + term
            parts.append(gate * actb[b * PEER_N_KEYS:(b + 1) * PEER_N_KEYS, :])
        p = jnp.concatenate(parts, axis=0) if len(parts) > 1 else parts[0]
        acc_ref[...] += _dot(vt_ref[:, rows], p)

    @pl.when(c == pl.num_programs(1) - 1)
    def _():
        o_ref[...] = x_ref[...] + acc_ref[...].T


def peer_dense(x, ht, u, vt, kap, alp, rho, bet, *, tm=512, te=1024, sub_experts=512):
    n_tok, dm = x.shape
    n_exp = u.shape[0]
    route_spec = pl.BlockSpec((PEER_HEADS, PEER_N_KEYS, tm), lambda t, c: (0, 0, t))
    return pl.pallas_call(
        functools.partial(_peer_dense_kernel, sub_experts=sub_experts),
        grid=(n_tok // tm, n_exp // te),
        in_specs=[
            pl.BlockSpec((tm, dm), lambda t, c: (t, 0)),
            pl.BlockSpec((dm, tm), lambda t, c: (0, t)),
            pl.BlockSpec((te, dm), lambda t, c: (c, 0)),
            pl.BlockSpec((dm, te), lambda t, c: (0, c)),
            route_spec, route_spec, route_spec, route_spec,
        ],
        out_specs=pl.BlockSpec((tm, dm), lambda t, c: (t, 0)),
        out_shape=jax.ShapeDtypeStruct(x.shape, x.dtype),
        scratch_shapes=[
            pltpu.VMEM((dm, tm), F32),
            pltpu.VMEM((PEER_HEADS, PEER_N_KEYS, tm), BF16),
            pltpu.VMEM((PEER_HEADS, PEER_N_KEYS, tm), BF16),
        ],
        input_output_aliases={0: 0},
        compiler_params=_cparams(("parallel", "arbitrary")),
        name="peer_dense",
    )(x, ht, u, vt, kap, alp, rho, bet)


def _ple_kernel(x_ref, p_ref, gain_ref, wg_ref, wp_ref, o_ref):
    x = x_ref[...]
    h = _rms(x, gain_ref[...]).astype(BF16)
    gate = jax.nn.sigmoid(_dot(h, wg_ref[...]))
    proj = _dot(p_ref[...].astype(BF16), wp_ref[...])
    o_ref[...] = x + gate * proj


def ple(x, p, gain, w_gate, w_proj, *, tm=512):
    n_tok, dm = x.shape
    dp = p.shape[1]
    return pl.pallas_call(
        _ple_kernel,
        grid=(n_tok // tm,),
        in_specs=[
            pl.BlockSpec((tm, dm), lambda t: (t, 0)),
            pl.BlockSpec((tm, dp), lambda t: (t, 0)),
            pl.BlockSpec((1, dm), lambda t: (0, 0)),
            pl.BlockSpec((dm, dm), lambda t: (0, 0)),
            pl.BlockSpec((dp, dm), lambda t: (0, 0)),
        ],
        out_specs=pl.BlockSpec((tm, dm), lambda t: (t, 0)),
        out_shape=jax.ShapeDtypeStruct(x.shape, x.dtype),
        input_output_aliases={0: 0},
        compiler_params=_cparams(("parallel",)),
        name="ple",
    )(x, p, gain, w_gate, w_proj)


def _head_norm_rope(y, gain, cosf, sinf):
    yn = _rms(y, gain)
    half = ROPE_DIMS // 2
    lane = lax.broadcasted_iota(jnp.int32, yn.shape, 1)
    rot = jnp.where(lane < half, pltpu.roll(yn, HEAD_DIM - half, 1), pltpu.roll(yn, half, 1))
    return yn * cosf + rot * sinf


def _kv_kernel(x_ref, gain_ref, w_ref, kgain_ref, cos_ref, sin_ref, k_ref, v_ref):
    dm = x_ref.shape[1]
    h = _rms(x_ref[...], gain_ref[...]).astype(BF16)
    kv = _dot(h, w_ref[...])
    v_ref[...] = kv[:, dm:].astype(BF16)
    cosf, sinf, kg = cos_ref[...], sin_ref[...], kgain_ref[...]
    for hd in range(KV_HEADS):
        sl = slice(hd * HEAD_DIM, (hd + 1) * HEAD_DIM)
        k_ref[:, sl] = _head_norm_rope(kv[:, sl], kg, cosf, sinf).astype(BF16)


def shared_kv(x, gain, w_kv, k_gain, cosf, sinf, *, tm=512):
    n_tok, dm = x.shape
    out = jax.ShapeDtypeStruct((n_tok, dm), BF16)
    return pl.pallas_call(
        _kv_kernel,
        grid=(n_tok // tm,),
        in_specs=[
            pl.BlockSpec((tm, dm), lambda t: (t, 0)),
            pl.BlockSpec((1, dm), lambda t: (0, 0)),
            pl.BlockSpec((dm, 2 * dm), lambda t: (0, 0)),
            pl.BlockSpec((1, HEAD_DIM), lambda t: (0, 0)),
            pl.BlockSpec((tm, HEAD_DIM), lambda t: (t, 0)),
            pl.BlockSpec((tm, HEAD_DIM), lambda t: (t, 0)),
        ],
        out_specs=[pl.BlockSpec((tm, dm), lambda t: (t, 0))] * 2,
        out_shape=[out, out],
        compiler_params=_cparams(("parallel",)),
        name="shared_kv",
    )(x, gain, w_kv, k_gain, cosf, sinf)


def _q_kernel(x_ref, gain_ref, w_ref, qgain_ref, cos_ref, sin_ref, q_ref):
    h = _rms(x_ref[...], gain_ref[...]).astype(BF16)
    q = _dot(h, w_ref[...])
    cosf, sinf, qg = cos_ref[...], sin_ref[...], qgain_ref[...]
    for hd in range(q.shape[1] // HEAD_DIM):
        sl = slice(hd * HEAD_DIM, (hd + 1) * HEAD_DIM)
        q_ref[:, sl] = _head_norm_rope(q[:, sl], qg, cosf, sinf).astype(BF16)


def query_proj(x, gain, w_q, q_gain, cosf, sinf, *, tm=512):
    n_tok, dm = x.shape
    nq = w_q.shape[1]
    return pl.pallas_call(
        _q_kernel,
        grid=(n_tok // tm,),
        in_specs=[
            pl.BlockSpec((tm, dm), lambda t: (t, 0)),
            pl.BlockSpec((1, dm), lambda t: (0, 0)),
            pl.BlockSpec((dm, nq), lambda t: (0, 0)),
            pl.BlockSpec((1, HEAD_DIM), lambda t: (0, 0)),
            pl.BlockSpec((tm, HEAD_DIM), lambda t: (t, 0)),
            pl.BlockSpec((tm, HEAD_DIM), lambda t: (t, 0)),
        ],
        out_specs=pl.BlockSpec((tm, nq), lambda t: (t, 0)),
        out_shape=jax.ShapeDtypeStruct((n_tok, nq), BF16),
        compiler_params=_cparams(("parallel",)),
        name="query_proj",
    )(x, gain, w_q, q_gain, cosf, sinf)


ATTN_BLOCK = 128


def _attn_block(q, k, v, mask):
    s = lax.dot_general(q, k, (((1,), (1,)), ((), ())), preferred_element_type=F32) * (HEAD_DIM ** -0.5)
    s = jnp.where(mask, s, NEG_INF)
    m = jnp.max(s, axis=-1, keepdims=True)
    e = jnp.exp(s - m)
    den = jnp.sum(e, axis=-1, keepdims=True)
    o = _dot((e / den).astype(BF16), v)
    return o, m + jnp.log(den)


def _attn_kernel(q_ref, k_ref, v_ref, o_ref, lse_ref):
    length = q_ref.shape[0]
    blk = ATTN_BLOCK
    n_blk = length // blk
    qi = lax.broadcasted_iota(jnp.int32, (blk, blk), 0)
    ki = lax.broadcasted_iota(jnp.int32, (blk, blk), 1)
    mask_first = ki <= qi
    qi2 = lax.broadcasted_iota(jnp.int32, (blk, 2 * blk), 0)
    ki2 = lax.broadcasted_iota(jnp.int32, (blk, 2 * blk), 1)
    dist = qi2 + blk - ki2
    mask_band = (dist >= 0) & (dist <= blk)
    lane = lax.broadcasted_iota(jnp.int32, (blk, LANES), 1)

    def one_block(q_rows, k_rows, mask):
        lse_tile = jnp.zeros((blk, LANES), F32)
        for hd in range(KV_HEADS):
            sl = slice(hd * HEAD_DIM, (hd + 1) * HEAD_DIM)
            o, lse = _attn_block(q_ref[q_rows, sl], k_ref[k_rows, sl], v_ref[k_rows, sl], mask)
            o_ref[q_rows, sl] = o.astype(o_ref.dtype)
            lse_tile = jnp.where(lane == hd, lse, lse_tile)
        lse_ref[q_rows, :] = lse_tile

    one_block(pl.ds(0, blk), pl.ds(0, blk), mask_first)

    def body(n, carry):
        q0 = pl.multiple_of(n * blk, blk)
        k0 = pl.multiple_of((n - 1) * blk, blk)
        one_block(pl.ds(q0, blk), pl.ds(k0, 2 * blk), mask_band)
        return carry

    lax.fori_loop(1, n_blk, body, 0)


def dilated_attention(q, k, v, *, batch, seq, group, dilation):
    width = k.shape[1]
    length = seq // dilation
    n_groups = q.shape[1] // width
    qv = q.reshape(batch, length, dilation * n_groups * width)
    kv_shape = (batch, length, dilation * width)
    o, lse = pl.pallas_call(
        _attn_kernel,
        grid=(batch, dilation),
        in_specs=[
            pl.BlockSpec((None, length, width), lambda b, r: (b, 0, r * n_groups + group)),
            pl.BlockSpec((None, length, width), lambda b, r: (b, 0, r)),
            pl.BlockSpec((None, length, width), lambda b, r: (b, 0, r)),
        ],
        out_specs=[
            pl.BlockSpec((None, length, width), lambda b, r: (b, 0, r)),
            pl.BlockSpec((None, length, LANES), lambda b, r: (b, 0, r)),
        ],
        out_shape=[
            jax.ShapeDtypeStruct(kv_shape, BF16),
            jax.ShapeDtypeStruct((batch, length, dilation * LANES), F32),
        ],
        compiler_params=_cparams(("parallel", "parallel")),
        name=f"dilated_attention_d{dilation}",
    )(qv, k.reshape(kv_shape), v.reshape(kv_shape))
    return o.reshape(batch * seq, width), lse.reshape(batch * seq, LANES)


def _attn_out_kernel(x_ref, o0_ref, o1_ref, o2_ref, l0_ref, l1_ref, l2_ref, expand_ref, wo_ref, out_ref):
    lses = [l0_ref[...], l1_ref[...], l2_ref[...]]
    m = jnp.maximum(jnp.maximum(lses[0], lses[1]), lses[2])
    es = [jnp.exp(l - m) for l in lses]
    inv = 1.0 / (es[0] + es[1] + es[2])
    expand = expand_ref[...]
    mix = None
    for e, o_ref in zip(es, (o0_ref, o1_ref, o2_ref)):
        w = e * inv
        w_hi = w.astype(BF16)
        w_lo = (w - w_hi.astype(F32)).astype(BF16)
        w_full = _dot(w_hi, expand) + _dot(w_lo, expand)
        term = w_full * o_ref[...].astype(F32)
        mix = term if mix is None else mix + term
    out_ref[...] = x_ref[...] + _dot(mix.astype(BF16), wo_ref[...])


def attention_out(x, outs, lses, expand, w_o, *, tm=512):
    n_tok, dm = x.shape
    row = lambda t: (t, 0)
    const = lambda t: (0, 0)
    return pl.pallas_call(
        _attn_out_kernel,
        grid=(n_tok // tm,),
        in_specs=[pl.BlockSpec((tm, dm), row)] * 4 + [pl.BlockSpec((tm, LANES), row)] * 3
        + [pl.BlockSpec((LANES, dm), const), pl.BlockSpec((dm, dm), const)],
        out_specs=pl.BlockSpec((tm, dm), row),
        out_shape=jax.ShapeDtypeStruct(x.shape, x.dtype),
        input_output_aliases={0: 0},
        compiler_params=_cparams(("parallel",)),
        name="attention_out",
    )(x, *outs, *lses, expand, w_o)


def _rope_tables(positions):
    half = ROPE_DIMS // 2
    inv_freq = 1.0 / (ROPE_THETA ** (jnp.arange(0, ROPE_DIMS, 2, dtype=F32) / ROPE_DIMS))
    ang = positions.astype(F32).reshape(-1, 1) * inv_freq
    cos, sin = jnp.cos(ang), jnp.sin(ang)
    pad = HEAD_DIM - ROPE_DIMS
    cosf = jnp.concatenate([cos, cos, jnp.ones((cos.shape[0], pad), F32)], axis=-1)
    sinf = jnp.concatenate([-sin, sin, jnp.zeros((sin.shape[0], pad), F32)], axis=-1)
    assert half * 2 == ROPE_DIMS
    return cosf, sinf


def kernel(x, p, positions, a_norm, a_w_in, a_conv, a_w_out, kv_norm, w_kv, k_norm, b_norm, b_w_q, q_norm, b_w_o, ffn_norm, peer_w_q, peer_sub_keys, peer_u, peer_v, ple_norm, ple_w_gate, ple_w_proj):
    batch, seq, dm = x.shape
    depth = p.shape[0]
    n_a = a_norm.shape[0]
    n_tok = batch * seq
    xf = x.reshape(n_tok, dm)
    cosf, sinf = _rope_tables(positions)
    head_of_lane = jnp.arange(dm) // HEAD_DIM
    expand = (jnp.arange(LANES)[:, None] == head_of_lane[None, :]).astype(BF16)

    k_sh = v_sh = None
    for i in range(depth):
        if i < n_a:
            xf = conv_mixer(xf, a_norm[i][None], a_w_in[i].astype(BF16), a_conv[i], a_w_out[i].astype(BF16), seq=seq)
        else:
            j = i - n_a
            q = query_proj(xf, b_norm[j][None], b_w_q[j].astype(BF16), q_norm[j][None], cosf, sinf)
            outs, lses = [], []
            for g, (_, dilation) in enumerate(DILATION_GROUPS):
                o_g, lse_g = dilated_attention(q, k_sh, v_sh, batch=batch, seq=seq, group=g, dilation=dilation)
                outs.append(o_g)
                lses.append(lse_g)
            xf = attention_out(xf, outs, lses, expand, b_w_o[j].astype(BF16))
        ht, kap, alp, rho, bet = peer_route(xf, ffn_norm[i][None], peer_w_q[i].T.astype(BF16),
                                            peer_sub_keys[i].astype(BF16))
        xf = peer_dense(xf, ht, peer_u[i].astype(BF16), peer_v[i].T.astype(BF16), kap, alp, rho, bet)
        xf = ple(xf, p[i].reshape(n_tok, -1), ple_norm[i][None], ple_w_gate[i].astype(BF16),
                 ple_w_proj[i].astype(BF16))
        if i == n_a - 1:
            k_sh, v_sh = shared_kv(xf, kv_norm[None], w_kv.astype(BF16), k_norm[None], cosf, sinf)
    return xf.reshape(batch, seq, dm)
```

```python
import functools

import jax
import jax.numpy as jnp
from jax import lax
from jax.experimental import pallas as pl
from jax.experimental.pallas import tpu as pltpu

F32 = jnp.float32
BF16 = jnp.bfloat16
U32 = jnp.uint32

LANES = 128
SUBLANES = 8
VMEM_LIMIT_BYTES = 56 * 1024 * 1024

NORM_EPS = 1e-6
HEAD_DIM = 128
KV_HEADS = 8
N_GROUPS = 3
ROPE_DIMS = HEAD_DIM // 4
ROPE_THETA = 500000.0
DILATION_GROUPS = ((128, 1), (512, 4), (2048, 16))
PEER_HEADS = 8
PEER_N_KEYS = 128
PEER_TOPK = 16
INV_SQRT2 = 0.7071067811865476
NEG_INF = float("-inf")


def _cparams(sem):
    return pltpu.CompilerParams(dimension_semantics=sem, vmem_limit_bytes=VMEM_LIMIT_BYTES)


def _rms(x, gain):
    ms = jnp.mean(x * x, axis=-1, keepdims=True)
    return x * lax.rsqrt(ms + NORM_EPS) * gain


def _dot(a, b):
    return jnp.dot(a, b, preferred_element_type=F32)


def _conv_mixer_kernel(x_ref, gain_ref, win_ref, conv_ref, wout_ref, o_ref, gbuf_ref):
    j = pl.program_id(1)
    tm, dm = x_ref.shape
    x = x_ref[...]
    h = _rms(x, gain_ref[...]).astype(BF16)
    bcu = _dot(h, win_ref[...])
    b_gate = bcu[:, :dm]
    g = bcu[:, dm:2 * dm] * bcu[:, 2 * dm:]

    @pl.when(j == 0)
    def _():
        gbuf_ref[0:SUBLANES, :] = jnp.zeros((SUBLANES, dm), F32)

    gbuf_ref[SUBLANES:SUBLANES + tm, :] = g
    g1 = gbuf_ref[SUBLANES - 1:SUBLANES - 1 + tm, :]
    g2 = gbuf_ref[SUBLANES - 2:SUBLANES - 2 + tm, :]
    w = conv_ref[...]
    z = w[0:1, :] * g2 + w[1:2, :] * g1 + w[2:3, :] * g
    gbuf_ref[0:SUBLANES, :] = g[tm - SUBLANES:, :]
    y = _dot((b_gate * z).astype(BF16), wout_ref[...])
    o_ref[...] = x + y


def conv_mixer(x, gain, w_in, conv_w, w_out, *, seq, tm=512):
    n_tok, dm = x.shape
    tiles_per_seq = seq // tm
    row = lambda b, j: (b * tiles_per_seq + j, 0)
    const = lambda b, j: (0, 0)
    return pl.pallas_call(
        _conv_mixer_kernel,
        grid=(n_tok // seq, tiles_per_seq),
        in_specs=[
            pl.BlockSpec((tm, dm), row),
            pl.BlockSpec((1, dm), const),
            pl.BlockSpec((dm, 3 * dm), const),
            pl.BlockSpec(conv_w.shape, const),
            pl.BlockSpec((dm, dm), const),
        ],
        out_specs=pl.BlockSpec((tm, dm), row),
        out_shape=jax.ShapeDtypeStruct(x.shape, x.dtype),
        scratch_shapes=[pltpu.VMEM((tm + SUBLANES, dm), F32)],
        input_output_aliases={0: 0},
        compiler_params=_cparams(("arbitrary", "arbitrary")),
        name="conv_mixer",
    )(x, gain, w_in, conv_w, w_out)


def _staircase():
    return [(k1, k2) for k1 in range(PEER_TOPK) for k2 in range(PEER_TOPK) if (k1 + 1) * (k2 + 1) <= PEER_TOPK]


def _tree_max(vals):
    vals = list(vals)
    while len(vals) > 1:
        nxt = [jnp.maximum(vals[a], vals[a + 1]) for a in range(0, len(vals) - 1, 2)]
        if len(vals) % 2:
            nxt.append(vals[-1])
        vals = nxt
    return vals[0]


def _dup_bf16_bits(v):
    bits = lax.bitcast_convert_type(v, U32)
    return bits | (bits >> 16)


def _peer_route_kernel(x_ref, gain_ref, wqt_ref, keys_ref,
                       ht_ref, kap_ref, alp_ref, rho_ref, bet_ref, s1_ref, s2_ref):
    tr = x_ref.shape[0]
    h = _rms(x_ref[...], gain_ref[...])
    ht = h.T.astype(BF16)
    ht_ref[...] = ht
    qt = _dot(wqt_ref[...], ht).astype(BF16)

    tops = {}
    for hd in range(PEER_HEADS):
        for half in range(2):
            r0 = (hd * 2 + half) * PEER_N_KEYS
            s = _dot(keys_ref[hd, half], qt[r0:r0 + PEER_N_KEYS, :])
            (s1_ref if half == 0 else s2_ref)[hd] = s
            work = s
            rank = jnp.full(s.shape, float(PEER_TOPK), F32)
            top = []
            for k in range(PEER_TOPK):
                m = jnp.max(work, axis=0, keepdims=True)
                eq = work == m
                if half == 1:
                    rank = jnp.where(eq, float(k), rank)
                work = jnp.where(eq, NEG_INF, work)
                top.append(m)
            tops[hd, half] = top
            if half == 1:
                rho_ref[hd] = rank.astype(BF16)

    a_k = [jnp.concatenate([tops[hd, 0][k] for hd in range(PEER_HEADS)], axis=0) for k in range(PEER_TOPK)]
    b_k = [jnp.concatenate([tops[hd, 1][k] for hd in range(PEER_HEADS)], axis=0) for k in range(PEER_TOPK)]
    cands = [a_k[k1] + b_k[k2] for k1, k2 in _staircase()]
    work = cands
    for _ in range(PEER_TOPK - 1):
        m = _tree_max(work)
        work = [jnp.where(c == m, NEG_INF, c) for c in work]
    tau = _tree_max(work)
    top_sum = cands[0]
    z = None
    for c in cands:
        e = jnp.where(c >= tau, jnp.exp(c - top_sum), 0.0)
        z = e if z is None else z + e
    inv_z = 1.0 / z

    for hd in range(PEER_HEADS):
        s1 = s1_ref[hd]
        s2 = s2_ref[hd]
        tau_h = tau[hd:hd + 1, :]
        kap = jnp.zeros(s1.shape, F32)
        for k2 in range(PEER_TOPK):
            kap = kap + jnp.where(s1 + tops[hd, 1][k2] >= tau_h, 1.0, 0.0)
        alp = jnp.exp(s1 - tops[hd, 0][0]).astype(BF16).astype(F32)
        bet = jnp.exp(s2 - tops[hd, 1][0]) * inv_z[hd:hd + 1, :]
        kap_ref[hd] = _dup_bf16_bits(kap)
        alp_ref[hd] = _dup_bf16_bits(alp)
        bet_ref[hd] = bet.astype(BF16)


def peer_route(x, gain, wq_t, sub_keys, *, tr=256):
    n_tok, dm = x.shape
    nq = wq_t.shape[0]
    route_spec = pl.BlockSpec((PEER_HEADS, PEER_N_KEYS, tr), lambda t: (0, 0, t))
    route_shape = (PEER_HEADS, PEER_N_KEYS, n_tok)
    return pl.pallas_call(
        _peer_route_kernel,
        grid=(n_tok // tr,),
        in_specs=[
            pl.BlockSpec((tr, dm), lambda t: (t, 0)),
            pl.BlockSpec((1, dm), lambda t: (0, 0)),
            pl.BlockSpec((nq, dm), lambda t: (0, 0)),
            pl.BlockSpec(sub_keys.shape, lambda t: (0, 0, 0, 0)),
        ],
        out_specs=[pl.BlockSpec((dm, tr), lambda t: (0, t)), route_spec, route_spec, route_spec, route_spec],
        out_shape=[
            jax.ShapeDtypeStruct((dm, n_tok), BF16),
            jax.ShapeDtypeStruct(route_shape, U32),
            jax.ShapeDtypeStruct(route_shape, U32),
            jax.ShapeDtypeStruct(route_shape, BF16),
            jax.ShapeDtypeStruct(route_shape, BF16),
        ],
        scratch_shapes=[pltpu.VMEM((PEER_HEADS, PEER_N_KEYS, tr), F32)] * 2,
        compiler_params=_cparams(("parallel",)),
        name="peer_route",
    )(x, gain, wq_t, sub_keys)


def _peer_dense_kernel(x_ref, ht_ref, u_ref, vt_ref, kap_ref, alp_ref, rho_in_ref, bet_in_ref,
                       o_ref, acc_ref, rho_ref, bet_ref, *, sub_experts, lookahead):
    c = pl.program_id(1)
    te = u_ref.shape[0]
    tm = ht_ref.shape[1]
    n_sub = te // sub_experts
    blocks_per_sub = sub_experts // PEER_N_KEYS
    blocks_per_chunk = te // PEER_N_KEYS
    assert blocks_per_chunk % SUBLANES == 0

    @pl.when(c == 0)
    def _():
        acc_ref[...] = jnp.zeros_like(acc_ref)
        rho_ref[...] = rho_in_ref[...]
        bet_ref[...] = bet_in_ref[...]

    ht = ht_ref[...]
    row0 = pl.multiple_of(c * blocks_per_chunk, SUBLANES)

    def expert_act(s):
        return _dot(u_ref[pl.ds(s * sub_experts, sub_experts), :], ht)

    def gated(s, act):
        act = 0.5 * act * (1.0 + lax.erf(act * INV_SQRT2))
        actb = act.astype(BF16)
        parts = []
        for b in range(blocks_per_sub):
            r = s * blocks_per_sub + b
            tile0 = row0 + (r // SUBLANES) * SUBLANES
            rr = r % SUBLANES
            gate = None
            for hd in range(PEER_HEADS):
                kap8 = kap_ref[hd, pl.ds(tile0, SUBLANES), :]
                alp8 = alp_ref[hd, pl.ds(tile0, SUBLANES), :]
                kap = pltpu.bitcast(jnp.broadcast_to(kap8[rr:rr + 1, :], (PEER_N_KEYS // 2, tm)), BF16)
                alp = pltpu.bitcast(jnp.broadcast_to(alp8[rr:rr + 1, :], (PEER_N_KEYS // 2, tm)), BF16)
                term = jnp.where(rho_ref[hd] < kap, alp * bet_ref[hd], jnp.zeros((), BF16))
                gate = term if gate is None else gate + term
            parts.append(gate * actb[b * PEER_N_KEYS:(b + 1) * PEER_N_KEYS, :])
        return jnp.concatenate(parts, axis=0) if len(parts) > 1 else parts[0]

    acts = {s: expert_act(s) for s in range(min(lookahead, n_sub))}
    for s in range(n_sub):
        if s + lookahead < n_sub:
            acts[s + lookahead] = expert_act(s + lookahead)
        p = gated(s, acts.pop(s))
        acc_ref[...] += _dot(vt_ref[:, pl.ds(s * sub_experts, sub_experts)], p)

    @pl.when(c == pl.num_programs(1) - 1)
    def _():
        o_ref[...] = x_ref[...] + acc_ref[...].T


def peer_dense(x, ht, u, vt, kap, alp, rho, bet, *, tm=512, te=2048, sub_experts=256, lookahead=3):
    n_tok, dm = x.shape
    n_exp = u.shape[0]
    route_spec = pl.BlockSpec((PEER_HEADS, PEER_N_KEYS, tm), lambda t, c: (0, 0, t))
    return pl.pallas_call(
        functools.partial(_peer_dense_kernel, sub_experts=sub_experts, lookahead=lookahead),
        grid=(n_tok // tm, n_exp // te),
        in_specs=[
            pl.BlockSpec((tm, dm), lambda t, c: (t, 0)),
            pl.BlockSpec((dm, tm), lambda t, c: (0, t)),
            pl.BlockSpec((te, dm), lambda t, c: (c, 0)),
            pl.BlockSpec((dm, te), lambda t, c: (0, c)),
            route_spec, route_spec, route_spec, route_spec,
        ],
        out_specs=pl.BlockSpec((tm, dm), lambda t, c: (t, 0)),
        out_shape=jax.ShapeDtypeStruct(x.shape, x.dtype),
        scratch_shapes=[
            pltpu.VMEM((dm, tm), F32),
            pltpu.VMEM((PEER_HEADS, PEER_N_KEYS, tm), BF16),
            pltpu.VMEM((PEER_HEADS, PEER_N_KEYS, tm), BF16),
        ],
        input_output_aliases={0: 0},
        compiler_params=_cparams(("parallel", "arbitrary")),
        name="peer_dense",
    )(x, ht, u, vt, kap, alp, rho, bet)


def _ple_kernel(x_ref, p_ref, gain_ref, wg_ref, wp_ref, o_ref):
    x = x_ref[...]
    h = _rms(x, gain_ref[...]).astype(BF16)
    gate = jax.nn.sigmoid(_dot(h, wg_ref[...]))
    proj = _dot(p_ref[...].astype(BF16), wp_ref[...])
    o_ref[...] = x + gate * proj


def ple(x, p, gain, w_gate, w_proj, *, tm=512):
    n_tok, dm = x.shape
    dp = p.shape[1]
    return pl.pallas_call(
        _ple_kernel,
        grid=(n_tok // tm,),
        in_specs=[
            pl.BlockSpec((tm, dm), lambda t: (t, 0)),
            pl.BlockSpec((tm, dp), lambda t: (t, 0)),
            pl.BlockSpec((1, dm), lambda t: (0, 0)),
            pl.BlockSpec((dm, dm), lambda t: (0, 0)),
            pl.BlockSpec((dp, dm), lambda t: (0, 0)),
        ],
        out_specs=pl.BlockSpec((tm, dm), lambda t: (t, 0)),
        out_shape=jax.ShapeDtypeStruct(x.shape, x.dtype),
        input_output_aliases={0: 0},
        compiler_params=_cparams(("parallel",)),
        name="ple",
    )(x, p, gain, w_gate, w_proj)


def _head_norm_rope(y, gain, cosf, sinf):
    yn = _rms(y, gain)
    half = ROPE_DIMS // 2
    lane = lax.broadcasted_iota(jnp.int32, yn.shape, 1)
    rot = jnp.where(lane < half, pltpu.roll(yn, HEAD_DIM - half, 1), pltpu.roll(yn, half, 1))
    return yn * cosf + rot * sinf


def _kv_kernel(x_ref, gain_ref, w_ref, kgain_ref, cos_ref, sin_ref, k_ref, v_ref):
    dm = x_ref.shape[1]
    h = _rms(x_ref[...], gain_ref[...]).astype(BF16)
    kv = _dot(h, w_ref[...])
    v_ref[...] = kv[:, dm:].astype(BF16)
    cosf, sinf, kg = cos_ref[...], sin_ref[...], kgain_ref[...]
    for hd in range(KV_HEADS):
        sl = slice(hd * HEAD_DIM, (hd + 1) * HEAD_DIM)
        k_ref[:, sl] = _head_norm_rope(kv[:, sl], kg, cosf, sinf).astype(BF16)


def shared_kv(x, gain, w_kv, k_gain, cosf, sinf, *, tm=512):
    n_tok, dm = x.shape
    out = jax.ShapeDtypeStruct((n_tok, dm), BF16)
    return pl.pallas_call(
        _kv_kernel,
        grid=(n_tok // tm,),
        in_specs=[
            pl.BlockSpec((tm, dm), lambda t: (t, 0)),
            pl.BlockSpec((1, dm), lambda t: (0, 0)),
            pl.BlockSpec((dm, 2 * dm), lambda t: (0, 0)),
            pl.BlockSpec((1, HEAD_DIM), lambda t: (0, 0)),
            pl.BlockSpec((tm, HEAD_DIM), lambda t: (t, 0)),
            pl.BlockSpec((tm, HEAD_DIM), lambda t: (t, 0)),
        ],
        out_specs=[pl.BlockSpec((tm, dm), lambda t: (t, 0))] * 2,
        out_shape=[out, out],
        compiler_params=_cparams(("parallel",)),
        name="shared_kv",
    )(x, gain, w_kv, k_gain, cosf, sinf)


def _q_kernel(x_ref, gain_ref, w_ref, qgain_ref, cos_ref, sin_ref, q_ref):
    h = _rms(x_ref[...], gain_ref[...]).astype(BF16)
    q = _dot(h, w_ref[...])
    cosf, sinf, qg = cos_ref[...], sin_ref[...], qgain_ref[...]
    for hd in range(q.shape[1] // HEAD_DIM):
        sl = slice(hd * HEAD_DIM, (hd + 1) * HEAD_DIM)
        q_ref[:, sl] = _head_norm_rope(q[:, sl], qg, cosf, sinf).astype(BF16)


def query_proj(x, gain, w_q, q_gain, cosf, sinf, *, tm=512):
    n_tok, dm = x.shape
    nq = w_q.shape[1]
    return pl.pallas_call(
        _q_kernel,
        grid=(n_tok // tm,),
        in_specs=[
            pl.BlockSpec((tm, dm), lambda t: (t, 0)),
            pl.BlockSpec((1, dm), lambda t: (0, 0)),
            pl.BlockSpec((dm, nq), lambda t: (0, 0)),
            pl.BlockSpec((1, HEAD_DIM), lambda t: (0, 0)),
            pl.BlockSpec((tm, HEAD_DIM), lambda t: (t, 0)),
            pl.BlockSpec((tm, HEAD_DIM), lambda t: (t, 0)),
        ],
        out_specs=pl.BlockSpec((tm, nq), lambda t: (t, 0)),
        out_shape=jax.ShapeDtypeStruct((n_tok, nq), BF16),
        compiler_params=_cparams(("parallel",)),
        name="query_proj",
    )(x, gain, w_q, q_gain, cosf, sinf)


ATTN_BLOCK = 128


def _attn_block(q, k, v, mask):
    s = lax.dot_general(q, k, (((1,), (1,)), ((), ())), preferred_element_type=F32) * (HEAD_DIM ** -0.5)
    s = jnp.where(mask, s, NEG_INF)
    m = jnp.max(s, axis=-1, keepdims=True)
    e = jnp.exp(s - m)
    den = jnp.sum(e, axis=-1, keepdims=True)
    o = _dot((e / den).astype(BF16), v)
    return o, m + jnp.log(den)


def _attn_kernel(q_ref, k_ref, v_ref, o_ref, lse_ref):
    length = q_ref.shape[0]
    blk = ATTN_BLOCK
    n_blk = length // blk
    qi = lax.broadcasted_iota(jnp.int32, (blk, blk), 0)
    ki = lax.broadcasted_iota(jnp.int32, (blk, blk), 1)
    mask_first = ki <= qi
    qi2 = lax.broadcasted_iota(jnp.int32, (blk, 2 * blk), 0)
    ki2 = lax.broadcasted_iota(jnp.int32, (blk, 2 * blk), 1)
    dist = qi2 + blk - ki2
    mask_band = (dist >= 0) & (dist <= blk)
    lane = lax.broadcasted_iota(jnp.int32, (blk, LANES), 1)

    def one_block(q_rows, k_rows, mask):
        lse_tile = jnp.zeros((blk, LANES), F32)
        for hd in range(KV_HEADS):
            sl = slice(hd * HEAD_DIM, (hd + 1) * HEAD_DIM)
            o, lse = _attn_block(q_ref[q_rows, sl], k_ref[k_rows, sl], v_ref[k_rows, sl], mask)
            o_ref[q_rows, sl] = o.astype(o_ref.dtype)
            lse_tile = jnp.where(lane == hd, lse, lse_tile)
        lse_ref[q_rows, :] = lse_tile

    one_block(pl.ds(0, blk), pl.ds(0, blk), mask_first)

    def body(n, carry):
        q0 = pl.multiple_of(n * blk, blk)
        k0 = pl.multiple_of((n - 1) * blk, blk)
        one_block(pl.ds(q0, blk), pl.ds(k0, 2 * blk), mask_band)
        return carry

    lax.fori_loop(1, n_blk, body, 0)


def dilated_attention(q, k, v, *, batch, seq, group, dilation):
    width = k.shape[1]
    length = seq // dilation
    n_groups = q.shape[1] // width
    qv = q.reshape(batch, length, dilation * n_groups * width)
    kv_shape = (batch, length, dilation * width)
    o, lse = pl.pallas_call(
        _attn_kernel,
        grid=(batch, dilation),
        in_specs=[
            pl.BlockSpec((None, length, width), lambda b, r: (b, 0, r * n_groups + group)),
            pl.BlockSpec((None, length, width), lambda b, r: (b, 0, r)),
            pl.BlockSpec((None, length, width), lambda b, r: (b, 0, r)),
        ],
        out_specs=[
            pl.BlockSpec((None, length, width), lambda b, r: (b, 0, r)),
            pl.BlockSpec((None, length, LANES), lambda b, r: (b, 0, r)),
        ],
        out_shape=[
            jax.ShapeDtypeStruct(kv_shape, BF16),
            jax.ShapeDtypeStruct((batch, length, dilation * LANES), F32),
        ],
        compiler_params=_cparams(("parallel", "parallel")),
        name=f"dilated_attention_d{dilation}",
    )(qv, k.reshape(kv_shape), v.reshape(kv_shape))
    return o.reshape(batch * seq, width), lse.reshape(batch * seq, LANES)


def _attn_out_kernel(x_ref, o0_ref, o1_ref, o2_ref, l0_ref, l1_ref, l2_ref, expand_ref, wo_ref, out_ref):
    lses = [l0_ref[...], l1_ref[...], l2_ref[...]]
    m = jnp.maximum(jnp.maximum(lses[0], lses[1]), lses[2])
    es = [jnp.exp(l - m) for l in lses]
    inv = 1.0 / (es[0] + es[1] + es[2])
    expand = expand_ref[...]
    mix = None
    for e, o_ref in zip(es, (o0_ref, o1_ref, o2_ref)):
        w = e * inv
        w_hi = w.astype(BF16)
        w_lo = (w - w_hi.astype(F32)).astype(BF16)
        w_full = _dot(w_hi, expand) + _dot(w_lo, expand)
        term = w_full * o_ref[...].astype(F32)
        mix = term if mix is None else mix + term
    out_ref[...] = x_ref[...] + _dot(mix.astype(BF16), wo_ref[...])


def attention_out(x, outs, lses, expand, w_o, *, tm=512):
    n_tok, dm = x.shape
    row = lambda t: (t, 0)
    const = lambda t: (0, 0)
    return pl.pallas_call(
        _attn_out_kernel,
        grid=(n_tok // tm,),
        in_specs=[pl.BlockSpec((tm, dm), row)] * 4 + [pl.BlockSpec((tm, LANES), row)] * 3
        + [pl.BlockSpec((LANES, dm), const), pl.BlockSpec((dm, dm), const)],
        out_specs=pl.BlockSpec((tm, dm), row),
        out_shape=jax.ShapeDtypeStruct(x.shape, x.dtype),
        input_output_aliases={0: 0},
        compiler_params=_cparams(("parallel",)),
        name="attention_out",
    )(x, *outs, *lses, expand, w_o)


def _rope_tables(positions):
    half = ROPE_DIMS // 2
    inv_freq = 1.0 / (ROPE_THETA ** (jnp.arange(0, ROPE_DIMS, 2, dtype=F32) / ROPE_DIMS))
    ang = positions.astype(F32).reshape(-1, 1) * inv_freq
    cos, sin = jnp.cos(ang), jnp.sin(ang)
    pad = HEAD_DIM - ROPE_DIMS
    cosf = jnp.concatenate([cos, cos, jnp.ones((cos.shape[0], pad), F32)], axis=-1)
    sinf = jnp.concatenate([-sin, sin, jnp.zeros((sin.shape[0], pad), F32)], axis=-1)
    assert half * 2 == ROPE_DIMS
    return cosf, sinf


def kernel(x, p, positions, a_norm, a_w_in, a_conv, a_w_out, kv_norm, w_kv, k_norm, b_norm, b_w_q, q_norm, b_w_o, ffn_norm, peer_w_q, peer_sub_keys, peer_u, peer_v, ple_norm, ple_w_gate, ple_w_proj):
    batch, seq, dm = x.shape
    depth = p.shape[0]
    n_a = a_norm.shape[0]
    n_tok = batch * seq
    xf = x.reshape(n_tok, dm)
    cosf, sinf = _rope_tables(positions)
    head_of_lane = jnp.arange(dm) // HEAD_DIM
    expand = (jnp.arange(LANES)[:, None] == head_of_lane[None, :]).astype(BF16)

    k_sh = v_sh = None
    for i in range(depth):
        if i < n_a:
            xf = conv_mixer(xf, a_norm[i][None], a_w_in[i].astype(BF16), a_conv[i], a_w_out[i].astype(BF16), seq=seq)
        else:
            j = i - n_a
            q = query_proj(xf, b_norm[j][None], b_w_q[j].astype(BF16), q_norm[j][None], cosf, sinf)
            outs, lses = [], []
            for g, (_, dilation) in enumerate(DILATION_GROUPS):
                o_g, lse_g = dilated_attention(q, k_sh, v_sh, batch=batch, seq=seq, group=g, dilation=dilation)
                outs.append(o_g)
                lses.append(lse_g)
            xf = attention_out(xf, outs, lses, expand, b_w_o[j].astype(BF16))
        ht, kap, alp, rho, bet = peer_route(xf, ffn_norm[i][None], peer_w_q[i].T.astype(BF16),
                                            peer_sub_keys[i].astype(BF16))
        xf = peer_dense(xf, ht, peer_u[i].astype(BF16), peer_v[i].T.astype(BF16), kap, alp, rho, bet)
        xf = ple(xf, p[i].reshape(n_tok, -1), ple_norm[i][None], ple_w_gate[i].astype(BF16),
                 ple_w_proj[i].astype(BF16))
        if i == n_a - 1:
            k_sh, v_sh = shared_kv(xf, kv_norm[None], w_kv.astype(BF16), k_norm[None], cosf, sinf)
    return xf.reshape(batch, seq, dm)
```

```python
import functools

import jax
import jax.numpy as jnp
from jax import lax
from jax.experimental import pallas as pl
from jax.experimental.pallas import tpu as pltpu

F32 = jnp.float32
BF16 = jnp.bfloat16
U32 = jnp.uint32

LANES = 128
SUBLANES = 8
VMEM_LIMIT_BYTES = 56 * 1024 * 1024

NORM_EPS = 1e-6
HEAD_DIM = 128
KV_HEADS = 8
N_GROUPS = 3
ROPE_DIMS = HEAD_DIM // 4
ROPE_THETA = 500000.0
DILATION_GROUPS = ((128, 1), (512, 4), (2048, 16))
PEER_HEADS = 8
PEER_N_KEYS = 128
PEER_TOPK = 16
INV_SQRT2 = 0.7071067811865476
NEG_INF = float("-inf")


def _cparams(sem):
    return pltpu.CompilerParams(dimension_semantics=sem, vmem_limit_bytes=VMEM_LIMIT_BYTES)


def _rms(x, gain):
    ms = jnp.mean(x * x, axis=-1, keepdims=True)
    return x * lax.rsqrt(ms + NORM_EPS) * gain


def _dot(a, b):
    return jnp.dot(a, b, preferred_element_type=F32)


def _conv_mixer_kernel(x_ref, gain_ref, win_ref, conv_ref, wout_ref, o_ref, gbuf_ref):
    j = pl.program_id(1)
    tm, dm = x_ref.shape
    x = x_ref[...]
    h = _rms(x, gain_ref[...]).astype(BF16)
    bcu = _dot(h, win_ref[...])
    b_gate = bcu[:, :dm]
    g = bcu[:, dm:2 * dm] * bcu[:, 2 * dm:]

    @pl.when(j == 0)
    def _():
        gbuf_ref[0:SUBLANES, :] = jnp.zeros((SUBLANES, dm), F32)

    gbuf_ref[SUBLANES:SUBLANES + tm, :] = g
    g1 = gbuf_ref[SUBLANES - 1:SUBLANES - 1 + tm, :]
    g2 = gbuf_ref[SUBLANES - 2:SUBLANES - 2 + tm, :]
    w = conv_ref[...]
    z = w[0:1, :] * g2 + w[1:2, :] * g1 + w[2:3, :] * g
    gbuf_ref[0:SUBLANES, :] = g[tm - SUBLANES:, :]
    y = _dot((b_gate * z).astype(BF16), wout_ref[...])
    o_ref[...] = x + y


def conv_mixer(x, gain, w_in, conv_w, w_out, *, seq, tm=512):
    n_tok, dm = x.shape
    tiles_per_seq = seq // tm
    row = lambda b, j: (b * tiles_per_seq + j, 0)
    const = lambda b, j: (0, 0)
    return pl.pallas_call(
        _conv_mixer_kernel,
        grid=(n_tok // seq, tiles_per_seq),
        in_specs=[
            pl.BlockSpec((tm, dm), row),
            pl.BlockSpec((1, dm), const),
            pl.BlockSpec((dm, 3 * dm), const),
            pl.BlockSpec(conv_w.shape, const),
            pl.BlockSpec((dm, dm), const),
        ],
        out_specs=pl.BlockSpec((tm, dm), row),
        out_shape=jax.ShapeDtypeStruct(x.shape, x.dtype),
        scratch_shapes=[pltpu.VMEM((tm + SUBLANES, dm), F32)],
        input_output_aliases={0: 0},
        compiler_params=_cparams(("arbitrary", "arbitrary")),
        name="conv_mixer",
    )(x, gain, w_in, conv_w, w_out)


def _staircase():
    return [(k1, k2) for k1 in range(PEER_TOPK) for k2 in range(PEER_TOPK) if (k1 + 1) * (k2 + 1) <= PEER_TOPK]


def _tree_max(vals):
    vals = list(vals)
    while len(vals) > 1:
        nxt = [jnp.maximum(vals[a], vals[a + 1]) for a in range(0, len(vals) - 1, 2)]
        if len(vals) % 2:
            nxt.append(vals[-1])
        vals = nxt
    return vals[0]


def _dup_bf16_bits(v):
    bits = lax.bitcast_convert_type(v, U32)
    return bits | (bits >> 16)


def _peer_route_kernel(x_ref, gain_ref, wqt_ref, keys_ref,
                       ht_ref, kap_ref, alp_ref, rho_ref, bet_ref, s1_ref, s2_ref):
    tr = x_ref.shape[0]
    h = _rms(x_ref[...], gain_ref[...])
    ht = h.T.astype(BF16)
    ht_ref[...] = ht
    qt = _dot(wqt_ref[...], ht).astype(BF16)

    tops = {}
    for hd in range(PEER_HEADS):
        for half in range(2):
            r0 = (hd * 2 + half) * PEER_N_KEYS
            s = _dot(keys_ref[hd, half], qt[r0:r0 + PEER_N_KEYS, :])
            (s1_ref if half == 0 else s2_ref)[hd] = s
            work = s
            rank = jnp.full(s.shape, float(PEER_TOPK), F32)
            top = []
            for k in range(PEER_TOPK):
                m = jnp.max(work, axis=0, keepdims=True)
                eq = work == m
                if half == 1:
                    rank = jnp.where(eq, float(k), rank)
                work = jnp.where(eq, NEG_INF, work)
                top.append(m)
            tops[hd, half] = top
            if half == 1:
                rho_ref[hd] = rank.astype(BF16)

    a_k = [jnp.concatenate([tops[hd, 0][k] for hd in range(PEER_HEADS)], axis=0) for k in range(PEER_TOPK)]
    b_k = [jnp.concatenate([tops[hd, 1][k] for hd in range(PEER_HEADS)], axis=0) for k in range(PEER_TOPK)]
    cands = [a_k[k1] + b_k[k2] for k1, k2 in _staircase()]
    work = cands
    for _ in range(PEER_TOPK - 1):
        m = _tree_max(work)
        work = [jnp.where(c == m, NEG_INF, c) for c in work]
    tau = _tree_max(work)
    top_sum = cands[0]
    z = None
    for c in cands:
        e = jnp.where(c >= tau, jnp.exp(c - top_sum), 0.0)
        z = e if z is None else z + e
    inv_z = 1.0 / z

    for hd in range(PEER_HEADS):
        s1 = s1_ref[hd]
        s2 = s2_ref[hd]
        tau_h = tau[hd:hd + 1, :]
        kap = jnp.zeros(s1.shape, F32)
        for k2 in range(PEER_TOPK):
            kap = kap + jnp.where(s1 + tops[hd, 1][k2] >= tau_h, 1.0, 0.0)
        alp = jnp.exp(s1 - tops[hd, 0][0]).astype(BF16).astype(F32)
        bet = jnp.exp(s2 - tops[hd, 1][0]) * inv_z[hd:hd + 1, :]
        kap_ref[hd] = _dup_bf16_bits(kap)
        alp_ref[hd] = _dup_bf16_bits(alp)
        bet_ref[hd] = bet.astype(BF16)


def peer_route(x, gain, wq_t, sub_keys, *, tr=256):
    n_tok, dm = x.shape
    nq = wq_t.shape[0]
    route_spec = pl.BlockSpec((PEER_HEADS, PEER_N_KEYS, tr), lambda t: (0, 0, t))
    route_shape = (PEER_HEADS, PEER_N_KEYS, n_tok)
    return pl.pallas_call(
        _peer_route_kernel,
        grid=(n_tok // tr,),
        in_specs=[
            pl.BlockSpec((tr, dm), lambda t: (t, 0)),
            pl.BlockSpec((1, dm), lambda t: (0, 0)),
            pl.BlockSpec((nq, dm), lambda t: (0, 0)),
            pl.BlockSpec(sub_keys.shape, lambda t: (0, 0, 0, 0)),
        ],
        out_specs=[pl.BlockSpec((dm, tr), lambda t: (0, t)), route_spec, route_spec, route_spec, route_spec],
        out_shape=[
            jax.ShapeDtypeStruct((dm, n_tok), BF16),
            jax.ShapeDtypeStruct(route_shape, U32),
            jax.ShapeDtypeStruct(route_shape, U32),
            jax.ShapeDtypeStruct(route_shape, BF16),
            jax.ShapeDtypeStruct(route_shape, BF16),
        ],
        scratch_shapes=[pltpu.VMEM((PEER_HEADS, PEER_N_KEYS, tr), F32)] * 2,
        compiler_params=_cparams(("parallel",)),
        name="peer_route",
    )(x, gain, wq_t, sub_keys)


def _peer_dense_kernel(x_ref, ht_ref, u_ref, vt_ref, kap_ref, alp_ref, rho_in_ref, bet_in_ref,
                       o_ref, acc_ref, rho_ref, bet_ref, *, sub_experts, lookahead):
    c = pl.program_id(1)
    te = u_ref.shape[0]
    tm = ht_ref.shape[1]
    n_sub = te // sub_experts
    blocks_per_sub = sub_experts // PEER_N_KEYS
    blocks_per_chunk = te // PEER_N_KEYS
    assert blocks_per_chunk % SUBLANES == 0

    @pl.when(c == 0)
    def _():
        acc_ref[...] = jnp.zeros_like(acc_ref)
        rho_ref[...] = rho_in_ref[...]
        bet_ref[...] = bet_in_ref[...]

    ht = ht_ref[...]
    row0 = pl.multiple_of(c * blocks_per_chunk, SUBLANES)

    def expert_act(s):
        return _dot(u_ref[pl.ds(s * sub_experts, sub_experts), :], ht)

    def gated(s, act):
        act = 0.5 * act * (1.0 + lax.erf(act * INV_SQRT2))
        actb = act.astype(BF16)
        parts = []
        for b in range(blocks_per_sub):
            r = s * blocks_per_sub + b
            tile0 = row0 + (r // SUBLANES) * SUBLANES
            rr = r % SUBLANES
            gate = None
            for hd in range(PEER_HEADS):
                kap8 = kap_ref[hd, pl.ds(tile0, SUBLANES), :]
                alp8 = alp_ref[hd, pl.ds(tile0, SUBLANES), :]
                kap = pltpu.bitcast(jnp.broadcast_to(kap8[rr:rr + 1, :], (PEER_N_KEYS // 2, tm)), BF16)
                alp = pltpu.bitcast(jnp.broadcast_to(alp8[rr:rr + 1, :], (PEER_N_KEYS // 2, tm)), BF16)
                term = jnp.where(rho_ref[hd] < kap, alp * bet_ref[hd], jnp.zeros((), BF16))
                gate = term if gate is None else gate + term
            parts.append(gate * actb[b * PEER_N_KEYS:(b + 1) * PEER_N_KEYS, :])
        return jnp.concatenate(parts, axis=0) if len(parts) > 1 else parts[0]

    acts = {s: expert_act(s) for s in range(min(lookahead, n_sub))}
    for s in range(n_sub):
        if s + lookahead < n_sub:
            acts[s + lookahead] = expert_act(s + lookahead)
        p = gated(s, acts.pop(s))
        acc_ref[...] += _dot(vt_ref[:, pl.ds(s * sub_experts, sub_experts)], p)

    @pl.when(c == pl.num_programs(1) - 1)
    def _():
        o_ref[...] = x_ref[...] + acc_ref[...].T


def peer_dense(x, ht, u, vt, kap, alp, rho, bet, *, tm=512, te=2048, sub_experts=256, lookahead=3):
    n_tok, dm = x.shape
    n_exp = u.shape[0]
    route_spec = pl.BlockSpec((PEER_HEADS, PEER_N_KEYS, tm), lambda t, c: (0, 0, t))
    return pl.pallas_call(
        functools.partial(_peer_dense_kernel, sub_experts=sub_experts, lookahead=lookahead),
        grid=(n_tok // tm, n_exp // te),
        in_specs=[
            pl.BlockSpec((tm, dm), lambda t, c: (t, 0)),
            pl.BlockSpec((dm, tm), lambda t, c: (0, t)),
            pl.BlockSpec((te, dm), lambda t, c: (c, 0)),
            pl.BlockSpec((dm, te), lambda t, c: (0, c)),
            route_spec, route_spec, route_spec, route_spec,
        ],
        out_specs=pl.BlockSpec((tm, dm), lambda t, c: (t, 0)),
        out_shape=jax.ShapeDtypeStruct(x.shape, x.dtype),
        scratch_shapes=[
            pltpu.VMEM((dm, tm), F32),
            pltpu.VMEM((PEER_HEADS, PEER_N_KEYS, tm), BF16),
            pltpu.VMEM((PEER_HEADS, PEER_N_KEYS, tm), BF16),
        ],
        input_output_aliases={0: 0},
        compiler_params=_cparams(("parallel", "arbitrary")),
        name="peer_dense",
    )(x, ht, u, vt, kap, alp, rho, bet)


def _ple_kernel(x_ref, p_ref, gain_ref, wg_ref, wp_ref, o_ref):
    x = x_ref[...]
    h = _rms(x, gain_ref[...]).astype(BF16)
    gate = jax.nn.sigmoid(_dot(h, wg_ref[...]))
    proj = _dot(p_ref[...].astype(BF16), wp_ref[...])
    o_ref[...] = x + gate * proj


def ple(x, p, gain, w_gate, w_proj, *, tm=512):
    n_tok, dm = x.shape
    dp = p.shape[1]
    return pl.pallas_call(
        _ple_kernel,
        grid=(n_tok // tm,),
        in_specs=[
            pl.BlockSpec((tm, dm), lambda t: (t, 0)),
            pl.BlockSpec((tm, dp), lambda t: (t, 0)),
            pl.BlockSpec((1, dm), lambda t: (0, 0)),
            pl.BlockSpec((dm, dm), lambda t: (0, 0)),
            pl.BlockSpec((dp, dm), lambda t: (0, 0)),
        ],
        out_specs=pl.BlockSpec((tm, dm), lambda t: (t, 0)),
        out_shape=jax.ShapeDtypeStruct(x.shape, x.dtype),
        input_output_aliases={0: 0},
        compiler_params=_cparams(("parallel",)),
        name="ple",
    )(x, p, gain, w_gate, w_proj)


def _head_norm_rope(y, gain, cosf, sinf):
    yn = _rms(y, gain)
    half = ROPE_DIMS // 2
    lane = lax.broadcasted_iota(jnp.int32, yn.shape, 1)
    rot = jnp.where(lane < half, pltpu.roll(yn, HEAD_DIM - half, 1), pltpu.roll(yn, half, 1))
    return yn * cosf + rot * sinf


def _store_residue_major(perm_ref, y, out_ref):
    d, m, w = out_ref.shape
    out_ref[...] = _dot(perm_ref[...], y).astype(out_ref.dtype).reshape(d, m, w)


def _kv_kernel(x_ref, gain_ref, w_ref, kgain_ref, cos_ref, sin_ref, perm1_ref, perm2_ref,
               k0_ref, v0_ref, k1_ref, v1_ref, k2_ref, v2_ref):
    dm = x_ref.shape[1]
    h = _rms(x_ref[...], gain_ref[...]).astype(BF16)
    kv = _dot(h, w_ref[...])
    v0_ref[...] = kv[:, dm:].astype(BF16)
    cosf, sinf, kg = cos_ref[...], sin_ref[...], kgain_ref[...]
    for hd in range(KV_HEADS):
        sl = slice(hd * HEAD_DIM, (hd + 1) * HEAD_DIM)
        k0_ref[:, sl] = _head_norm_rope(kv[:, sl], kg, cosf, sinf).astype(BF16)
    for perm_ref, k_ref, v_ref in ((perm1_ref, k1_ref, v1_ref), (perm2_ref, k2_ref, v2_ref)):
        _store_residue_major(perm_ref, k0_ref[...], k_ref)
        _store_residue_major(perm_ref, v0_ref[...], v_ref)


def _residue_major_spec(tm, dilation, width, tiles_per_seq):
    return pl.BlockSpec((None, dilation, None, tm // dilation, width),
                        lambda t: (t // tiles_per_seq, 0, t % tiles_per_seq, 0, 0))


def _residue_major_shape(batch, seq, tm, dilation, width, dtype):
    return jax.ShapeDtypeStruct((batch, dilation, seq // tm, tm // dilation, width), dtype)


def shared_kv(x, gain, w_kv, k_gain, cosf, sinf, perms, *, batch, seq, tm=512):
    n_tok, dm = x.shape
    tps = seq // tm
    row = pl.BlockSpec((tm, dm), lambda t: (t, 0))
    dils = [d for _, d in DILATION_GROUPS[1:]]
    const2 = lambda t: (0, 0)
    outs = pl.pallas_call(
        _kv_kernel,
        grid=(n_tok // tm,),
        in_specs=[
            row,
            pl.BlockSpec((1, dm), const2),
            pl.BlockSpec((dm, 2 * dm), const2),
            pl.BlockSpec((1, HEAD_DIM), const2),
            pl.BlockSpec((tm, HEAD_DIM), lambda t: (t, 0)),
            pl.BlockSpec((tm, HEAD_DIM), lambda t: (t, 0)),
            pl.BlockSpec((tm, tm), const2),
            pl.BlockSpec((tm, tm), const2),
        ],
        out_specs=[row, row] + [_residue_major_spec(tm, d, dm, tps) for d in dils for _ in range(2)],
        out_shape=[jax.ShapeDtypeStruct((n_tok, dm), BF16)] * 2
        + [_residue_major_shape(batch, seq, tm, d, dm, BF16) for d in dils for _ in range(2)],
        compiler_params=_cparams(("parallel",)),
        name="shared_kv",
    )(x, gain, w_kv, k_gain, cosf, sinf, *perms)
    return [(outs[0], outs[1]), (outs[2], outs[3]), (outs[4], outs[5])]


def _q_kernel(x_ref, gain_ref, w_ref, qgain_ref, cos_ref, sin_ref, perm1_ref, perm2_ref,
              q0_ref, q1_ref, q2_ref, qbuf_ref):
    dm = x_ref.shape[1]
    h = _rms(x_ref[...], gain_ref[...]).astype(BF16)
    q = _dot(h, w_ref[...])
    cosf, sinf, qg = cos_ref[...], sin_ref[...], qgain_ref[...]
    for hd in range(q.shape[1] // HEAD_DIM):
        sl = slice(hd * HEAD_DIM, (hd + 1) * HEAD_DIM)
        y = _head_norm_rope(q[:, sl], qg, cosf, sinf).astype(BF16)
        if hd < KV_HEADS:
            q0_ref[:, sl] = y
        else:
            qbuf_ref[:, hd * HEAD_DIM - dm:(hd + 1) * HEAD_DIM - dm] = y
    _store_residue_major(perm1_ref, qbuf_ref[:, :dm], q1_ref)
    _store_residue_major(perm2_ref, qbuf_ref[:, dm:], q2_ref)


def query_proj(x, gain, w_q, q_gain, cosf, sinf, perms, *, batch, seq, tm=512):
    n_tok, dm = x.shape
    nq = w_q.shape[1]
    tps = seq // tm
    dils = [d for _, d in DILATION_GROUPS[1:]]
    const2 = lambda t: (0, 0)
    return pl.pallas_call(
        _q_kernel,
        grid=(n_tok // tm,),
        in_specs=[
            pl.BlockSpec((tm, dm), lambda t: (t, 0)),
            pl.BlockSpec((1, dm), const2),
            pl.BlockSpec((dm, nq), const2),
            pl.BlockSpec((1, HEAD_DIM), const2),
            pl.BlockSpec((tm, HEAD_DIM), lambda t: (t, 0)),
            pl.BlockSpec((tm, HEAD_DIM), lambda t: (t, 0)),
            pl.BlockSpec((tm, tm), const2),
            pl.BlockSpec((tm, tm), const2),
        ],
        out_specs=[pl.BlockSpec((tm, dm), lambda t: (t, 0))] + [_residue_major_spec(tm, d, dm, tps) for d in dils],
        out_shape=[jax.ShapeDtypeStruct((n_tok, dm), BF16)]
        + [_residue_major_shape(batch, seq, tm, d, dm, BF16) for d in dils],
        scratch_shapes=[pltpu.VMEM((tm, nq - dm), BF16)],
        compiler_params=_cparams(("parallel",)),
        name="query_proj",
    )(x, gain, w_q, q_gain, cosf, sinf, *perms)


ATTN_BLOCK = 128


def _attn_scores(q, k):
    return lax.dot_general(q, k, (((1,), (1,)), ((), ())), preferred_element_type=F32) * (HEAD_DIM ** -0.5)


def _attn_softmax(s, mask):
    s = jnp.where(mask, s, NEG_INF)
    m = jnp.max(s, axis=-1, keepdims=True)
    e = jnp.exp(s - m)
    den = jnp.sum(e, axis=-1, keepdims=True)
    return (e / den).astype(BF16), m + jnp.log(den)


def _attn_kernel(q_ref, k_ref, v_ref, o_ref, lse_ref):
    length = q_ref.shape[0]
    blk = ATTN_BLOCK
    n_blk = length // blk
    qi = lax.broadcasted_iota(jnp.int32, (blk, blk), 0)
    ki = lax.broadcasted_iota(jnp.int32, (blk, blk), 1)
    mask_first = ki <= qi
    qi2 = lax.broadcasted_iota(jnp.int32, (blk, 2 * blk), 0)
    ki2 = lax.broadcasted_iota(jnp.int32, (blk, 2 * blk), 1)
    dist = qi2 + blk - ki2
    mask_band = (dist >= 0) & (dist <= blk)
    lane = lax.broadcasted_iota(jnp.int32, (blk, LANES), 1)

    def one_block(q_rows, k_rows, mask):
        heads = [slice(hd * HEAD_DIM, (hd + 1) * HEAD_DIM) for hd in range(KV_HEADS)]
        scores = [_attn_scores(q_ref[q_rows, sl], k_ref[k_rows, sl]) for sl in heads]
        probs = [_attn_softmax(s, mask) for s in scores]
        lse_tile = jnp.zeros((blk, LANES), F32)
        for hd, sl in enumerate(heads):
            p, lse = probs[hd]
            o_ref[q_rows, sl] = _dot(p, v_ref[k_rows, sl]).astype(o_ref.dtype)
            lse_tile = jnp.where(lane == hd, lse, lse_tile)
        lse_ref[q_rows, :] = lse_tile

    one_block(pl.ds(0, blk), pl.ds(0, blk), mask_first)

    def body(n, carry):
        q0 = pl.multiple_of(n * blk, blk)
        k0 = pl.multiple_of((n - 1) * blk, blk)
        one_block(pl.ds(q0, blk), pl.ds(k0, 2 * blk), mask_band)
        return carry

    lax.fori_loop(1, n_blk, body, 0)


def dilated_attention(q, k, v, *, batch, seq, dilation):
    width = k.shape[-1]
    length = seq // dilation
    spec = pl.BlockSpec((None, None, length, width), lambda b, r: (b, r, 0, 0))
    return pl.pallas_call(
        _attn_kernel,
        grid=(batch, dilation),
        in_specs=[spec, spec, spec],
        out_specs=[spec, pl.BlockSpec((None, None, length, LANES), lambda b, r: (b, r, 0, 0))],
        out_shape=[
            jax.ShapeDtypeStruct((batch, dilation, length, width), BF16),
            jax.ShapeDtypeStruct((batch, dilation, length, LANES), F32),
        ],
        compiler_params=_cparams(("parallel", "parallel")),
        name=f"dilated_attention_d{dilation}",
    )(q, k, v)


def _split3_bf16(v):
    hi = v.astype(BF16)
    r1 = v - hi.astype(F32)
    mid = r1.astype(BF16)
    lo = (r1 - mid.astype(F32)).astype(BF16)
    return hi, mid, lo


def _attn_out_kernel(x_ref, o0_ref, o1_ref, o2_ref, l0_ref, l1_ref, l2_ref, unperm1_ref, unperm2_ref,
                     expand_ref, wo_ref, out_ref):
    tm, dm = x_ref.shape
    outs = [o0_ref[...].astype(F32)]
    lses = [l0_ref[...]]
    for o_ref, l_ref, unperm_ref in ((o1_ref, l1_ref, unperm1_ref), (o2_ref, l2_ref, unperm2_ref)):
        unperm = unperm_ref[...]
        outs.append(_dot(unperm, o_ref[...].reshape(tm, dm)))
        pieces = _split3_bf16(l_ref[...].reshape(tm, LANES))
        lses.append(_dot(unperm, pieces[0]) + _dot(unperm, pieces[1]) + _dot(unperm, pieces[2]))
    m = jnp.maximum(jnp.maximum(lses[0], lses[1]), lses[2])
    es = [jnp.exp(l - m) for l in lses]
    inv = 1.0 / (es[0] + es[1] + es[2])
    expand = expand_ref[...]
    mix = None
    for e, o in zip(es, outs):
        w = e * inv
        w_hi = w.astype(BF16)
        w_lo = (w - w_hi.astype(F32)).astype(BF16)
        w_full = _dot(w_hi, expand) + _dot(w_lo, expand)
        term = w_full * o
        mix = term if mix is None else mix + term
    out_ref[...] = x_ref[...] + _dot(mix.astype(BF16), wo_ref[...])


def attention_out(x, outs, lses, unperms, expand, w_o, *, batch, seq, tm=512):
    n_tok, dm = x.shape
    tps = seq // tm
    dils = [d for _, d in DILATION_GROUPS]
    row = lambda t: (t, 0)
    const = lambda t: (0, 0)
    o_specs = [pl.BlockSpec((tm, dm), row)] + [_residue_major_spec(tm, d, dm, tps) for d in dils[1:]]
    l_specs = [pl.BlockSpec((tm, LANES), row)] + [_residue_major_spec(tm, d, LANES, tps) for d in dils[1:]]
    o_args = [outs[0].reshape(n_tok, dm)] + [o.reshape(batch, d, tps, tm // d, dm) for o, d in zip(outs[1:], dils[1:])]
    l_args = [lses[0].reshape(n_tok, LANES)] + [l.reshape(batch, d, tps, tm // d, LANES)
                                                for l, d in zip(lses[1:], dils[1:])]
    return pl.pallas_call(
        _attn_out_kernel,
        grid=(n_tok // tm,),
        in_specs=[pl.BlockSpec((tm, dm), row)] + o_specs + l_specs
        + [pl.BlockSpec((tm, tm), const)] * 2 + [pl.BlockSpec((LANES, dm), const), pl.BlockSpec((dm, dm), const)],
        out_specs=pl.BlockSpec((tm, dm), row),
        out_shape=jax.ShapeDtypeStruct(x.shape, x.dtype),
        input_output_aliases={0: 0},
        compiler_params=_cparams(("parallel",)),
        name="attention_out",
    )(x, *o_args, *l_args, *unperms, expand, w_o)


def _rope_tables(positions):
    half = ROPE_DIMS // 2
    inv_freq = 1.0 / (ROPE_THETA ** (jnp.arange(0, ROPE_DIMS, 2, dtype=F32) / ROPE_DIMS))
    ang = positions.astype(F32).reshape(-1, 1) * inv_freq
    cos, sin = jnp.cos(ang), jnp.sin(ang)
    pad = HEAD_DIM - ROPE_DIMS
    cosf = jnp.concatenate([cos, cos, jnp.ones((cos.shape[0], pad), F32)], axis=-1)
    sinf = jnp.concatenate([-sin, sin, jnp.zeros((sin.shape[0], pad), F32)], axis=-1)
    assert half * 2 == ROPE_DIMS
    return cosf, sinf


def _residue_perm(tm, dilation):
    m = tm // dilation
    idx = jnp.arange(tm)
    src_row = (idx % m) * dilation + idx // m
    return (src_row[:, None] == jnp.arange(tm)[None, :]).astype(BF16)


def kernel(x, p, positions, a_norm, a_w_in, a_conv, a_w_out, kv_norm, w_kv, k_norm, b_norm, b_w_q, q_norm, b_w_o, ffn_norm, peer_w_q, peer_sub_keys, peer_u, peer_v, ple_norm, ple_w_gate, ple_w_proj):
    batch, seq, dm = x.shape
    depth = p.shape[0]
    n_a = a_norm.shape[0]
    n_tok = batch * seq
    tm = 512
    xf = x.reshape(n_tok, dm)
    cosf, sinf = _rope_tables(positions)
    head_of_lane = jnp.arange(dm) // HEAD_DIM
    expand = (jnp.arange(LANES)[:, None] == head_of_lane[None, :]).astype(BF16)
    dils = [d for _, d in DILATION_GROUPS]
    assert dils[0] == 1 and all(w // d == ATTN_BLOCK for w, d in DILATION_GROUPS)
    perms = [_residue_perm(tm, d) for d in dils[1:]]
    unperms = [pm.T for pm in perms]

    def residue_major(a, d):
        return a.reshape(batch, d, seq // d, a.shape[-1])

    kvs = None
    for i in range(depth):
        if i < n_a:
            xf = conv_mixer(xf, a_norm[i][None], a_w_in[i].astype(BF16), a_conv[i], a_w_out[i].astype(BF16), seq=seq)
        else:
            j = i - n_a
            qs = query_proj(xf, b_norm[j][None], b_w_q[j].astype(BF16), q_norm[j][None], cosf, sinf, perms,
                            batch=batch, seq=seq, tm=tm)
            outs, lses = [], []
            for q_g, (k_g, v_g), d in zip(qs, kvs, dils):
                o_g, lse_g = dilated_attention(residue_major(q_g, d), residue_major(k_g, d), residue_major(v_g, d),
                                               batch=batch, seq=seq, dilation=d)
                outs.append(o_g)
                lses.append(lse_g)
            xf = attention_out(xf, outs, lses, unperms, expand, b_w_o[j].astype(BF16), batch=batch, seq=seq, tm=tm)
        ht, kap, alp, rho, bet = peer_route(xf, ffn_norm[i][None], peer_w_q[i].T.astype(BF16),
                                            peer_sub_keys[i].astype(BF16))
        xf = peer_dense(xf, ht, peer_u[i].astype(BF16), peer_v[i].T.astype(BF16), kap, alp, rho, bet)
        xf = ple(xf, p[i].reshape(n_tok, -1), ple_norm[i][None], ple_w_gate[i].astype(BF16),
                 ple_w_proj[i].astype(BF16))
        if i == n_a - 1:
            kvs = shared_kv(xf, kv_norm[None], w_kv.astype(BF16), k_norm[None], cosf, sinf, perms,
                            batch=batch, seq=seq, tm=tm)
    return xf.reshape(batch, seq, dm)
```

```python
import functools

import jax
import jax.numpy as jnp
from jax import lax
from jax.experimental import pallas as pl
from jax.experimental.pallas import tpu as pltpu

F32 = jnp.float32
BF16 = jnp.bfloat16
U32 = jnp.uint32

LANES = 128
SUBLANES = 8
VMEM_LIMIT_BYTES = 56 * 1024 * 1024

NORM_EPS = 1e-6
HEAD_DIM = 128
KV_HEADS = 8
N_GROUPS = 3
ROPE_DIMS = HEAD_DIM // 4
ROPE_THETA = 500000.0
DILATION_GROUPS = ((128, 1), (512, 4), (2048, 16))
PEER_HEADS = 8
PEER_N_KEYS = 128
PEER_TOPK = 16
INV_SQRT2 = 0.7071067811865476
NEG_INF = float("-inf")


def _cparams(sem):
    return pltpu.CompilerParams(dimension_semantics=sem, vmem_limit_bytes=VMEM_LIMIT_BYTES)


def _rms(x, gain):
    ms = jnp.mean(x * x, axis=-1, keepdims=True)
    return x * lax.rsqrt(ms + NORM_EPS) * gain


def _dot(a, b):
    return jnp.dot(a, b, preferred_element_type=F32)


def _conv_mixer_kernel(x_ref, gain_ref, win_ref, conv_ref, wout_ref, o_ref, gbuf_ref):
    j = pl.program_id(1)
    tm, dm = x_ref.shape
    x = x_ref[...]
    h = _rms(x, gain_ref[...]).astype(BF16)
    bcu = _dot(h, win_ref[...])
    b_gate = bcu[:, :dm]
    g = bcu[:, dm:2 * dm] * bcu[:, 2 * dm:]

    @pl.when(j == 0)
    def _():
        gbuf_ref[0:SUBLANES, :] = jnp.zeros((SUBLANES, dm), F32)

    gbuf_ref[SUBLANES:SUBLANES + tm, :] = g
    g1 = gbuf_ref[SUBLANES - 1:SUBLANES - 1 + tm, :]
    g2 = gbuf_ref[SUBLANES - 2:SUBLANES - 2 + tm, :]
    w = conv_ref[...]
    z = w[0:1, :] * g2 + w[1:2, :] * g1 + w[2:3, :] * g
    gbuf_ref[0:SUBLANES, :] = g[tm - SUBLANES:, :]
    y = _dot((b_gate * z).astype(BF16), wout_ref[...])
    o_ref[...] = x + y


def conv_mixer(x, gain, w_in, conv_w, w_out, *, seq, tm=512):
    n_tok, dm = x.shape
    tiles_per_seq = seq // tm
    row = lambda b, j: (b * tiles_per_seq + j, 0)
    const = lambda b, j: (0, 0)
    return pl.pallas_call(
        _conv_mixer_kernel,
        grid=(n_tok // seq, tiles_per_seq),
        in_specs=[
            pl.BlockSpec((tm, dm), row),
            pl.BlockSpec((1, dm), const),
            pl.BlockSpec((dm, 3 * dm), const),
            pl.BlockSpec(conv_w.shape, const),
            pl.BlockSpec((dm, dm), const),
        ],
        out_specs=pl.BlockSpec((tm, dm), row),
        out_shape=jax.ShapeDtypeStruct(x.shape, x.dtype),
        scratch_shapes=[pltpu.VMEM((tm + SUBLANES, dm), F32)],
        input_output_aliases={0: 0},
        compiler_params=_cparams(("arbitrary", "arbitrary")),
        name="conv_mixer",
    )(x, gain, w_in, conv_w, w_out)


def _staircase():
    return [(k1, k2) for k1 in range(PEER_TOPK) for k2 in range(PEER_TOPK) if (k1 + 1) * (k2 + 1) <= PEER_TOPK]


def _tree_max(vals):
    vals = list(vals)
    while len(vals) > 1:
        nxt = [jnp.maximum(vals[a], vals[a + 1]) for a in range(0, len(vals) - 1, 2)]
        if len(vals) % 2:
            nxt.append(vals[-1])
        vals = nxt
    return vals[0]


def _dup_bf16_bits(v):
    bits = lax.bitcast_convert_type(v, U32)
    return bits | (bits >> 16)


def _sorting_network(n):
    pairs = []
    p = 1
    while p < n:
        k = p
        while k >= 1:
            for j in range(k % p, n - k, 2 * k):
                for i in range(min(k, n - j - k)):
                    if (i + j) // (2 * p) == (i + j + k) // (2 * p):
                        pairs.append((i + j, i + j + k))
            k //= 2
        p *= 2
    return pairs


def _compare_exchange(xs, i, j):
    xs[i], xs[j] = jnp.maximum(xs[i], xs[j]), jnp.minimum(xs[i], xs[j])


def _top_sorted(s3):
    n = PEER_TOPK
    assert s3.shape[0] == n and s3.shape[1] == SUBLANES
    xs = [s3[v] for v in range(n)]
    for i, j in _sorting_network(n):
        _compare_exchange(xs, i, j)
    shift = SUBLANES // 2
    while shift >= 1:
        ys = [pltpu.roll(x, shift, 0) for x in xs]
        xs = [jnp.maximum(xs[k], ys[n - 1 - k]) for k in range(n)]
        d = n // 2
        while d >= 1:
            for k in range(n):
                if k & d == 0:
                    _compare_exchange(xs, k, k + d)
            d //= 2
        shift //= 2
    return xs


def _peer_route_kernel(x_ref, gain_ref, wqt_ref, keys_ref,
                       ht_ref, kap_ref, alp_ref, rho_ref, bet_ref, s1_ref, s2_ref):
    tr = x_ref.shape[0]
    groups = PEER_N_KEYS // SUBLANES
    h = _rms(x_ref[...], gain_ref[...])
    ht = h.T.astype(BF16)
    ht_ref[...] = ht
    qt = _dot(wqt_ref[...], ht).astype(BF16)

    tops = {}
    for hd in range(PEER_HEADS):
        for half in range(2):
            r0 = (hd * 2 + half) * PEER_N_KEYS
            s = _dot(keys_ref[hd, half], qt[r0:r0 + PEER_N_KEYS, :])
            (s1_ref if half == 0 else s2_ref)[hd] = s
            tops[hd, half] = _top_sorted(s.reshape(groups, SUBLANES, tr))

    a_k = [jnp.concatenate([tops[hd, 0][k][0:1, :] for hd in range(PEER_HEADS)], axis=0) for k in range(PEER_TOPK)]
    b_k = [jnp.concatenate([tops[hd, 1][k][0:1, :] for hd in range(PEER_HEADS)], axis=0) for k in range(PEER_TOPK)]
    stairs = _staircase()
    cands = [a_k[k1] + b_k[k2] for k1, k2 in stairs]
    work = cands
    for _ in range(PEER_TOPK - 1):
        m = _tree_max(work)
        work = [jnp.where(c == m, NEG_INF, c) for c in work]
    tau = _tree_max(work)
    top_sum = cands[0]
    z = None
    theta = [None] * PEER_TOPK
    for (k1, k2), c in zip(stairs, cands):
        sel = c >= tau
        e = jnp.where(sel, jnp.exp(c - top_sum), 0.0)
        z = e if z is None else z + e
        t = jnp.where(sel, a_k[k1], jnp.inf)
        theta[k2] = t if theta[k2] is None else jnp.minimum(theta[k2], t)
    inv_z = 1.0 / z

    for hd in range(PEER_HEADS):
        s1 = s1_ref[hd].reshape(groups, SUBLANES, tr)
        s2 = s2_ref[hd].reshape(groups, SUBLANES, tr)
        row = lambda v: jnp.broadcast_to(v[hd:hd + 1, :], (SUBLANES, tr))[None]
        kap = jnp.zeros(s1.shape, F32)
        for k2 in range(PEER_TOPK):
            kap = jnp.where(s1 >= row(theta[k2]), float(k2 + 1), kap)
        rho = jnp.full(s2.shape, float(PEER_TOPK), F32)
        for k in reversed(range(PEER_TOPK)):
            rho = jnp.where(s2 >= tops[hd, 1][k][None], float(k), rho)
        alp = jnp.exp(s1 - tops[hd, 0][0][None]).astype(BF16).astype(F32)
        bet = jnp.exp(s2 - tops[hd, 1][0][None]) * row(inv_z)
        kap_ref[hd] = _dup_bf16_bits(kap).reshape(PEER_N_KEYS, tr)
        alp_ref[hd] = _dup_bf16_bits(alp).reshape(PEER_N_KEYS, tr)
        rho_ref[hd] = rho.reshape(PEER_N_KEYS, tr).astype(BF16)
        bet_ref[hd] = bet.reshape(PEER_N_KEYS, tr).astype(BF16)


def peer_route(x, gain, wq_t, sub_keys, *, tr=256):
    n_tok, dm = x.shape
    nq = wq_t.shape[0]
    route_spec = pl.BlockSpec((PEER_HEADS, PEER_N_KEYS, tr), lambda t: (0, 0, t))
    route_shape = (PEER_HEADS, PEER_N_KEYS, n_tok)
    return pl.pallas_call(
        _peer_route_kernel,
        grid=(n_tok // tr,),
        in_specs=[
            pl.BlockSpec((tr, dm), lambda t: (t, 0)),
            pl.BlockSpec((1, dm), lambda t: (0, 0)),
            pl.BlockSpec((nq, dm), lambda t: (0, 0)),
            pl.BlockSpec(sub_keys.shape, lambda t: (0, 0, 0, 0)),
        ],
        out_specs=[pl.BlockSpec((dm, tr), lambda t: (0, t)), route_spec, route_spec, route_spec, route_spec],
        out_shape=[
            jax.ShapeDtypeStruct((dm, n_tok), BF16),
            jax.ShapeDtypeStruct(route_shape, U32),
            jax.ShapeDtypeStruct(route_shape, U32),
            jax.ShapeDtypeStruct(route_shape, BF16),
            jax.ShapeDtypeStruct(route_shape, BF16),
        ],
        scratch_shapes=[pltpu.VMEM((PEER_HEADS, PEER_N_KEYS, tr), F32)] * 2,
        compiler_params=_cparams(("parallel",)),
        name="peer_route",
    )(x, gain, wq_t, sub_keys)


def _peer_dense_kernel(x_ref, ht_ref, u_ref, vt_ref, kap_ref, alp_ref, rho_in_ref, bet_in_ref,
                       o_ref, acc_ref, rho_ref, bet_ref, *, sub_experts, lookahead):
    c = pl.program_id(1)
    te = u_ref.shape[0]
    tm = ht_ref.shape[1]
    n_sub = te // sub_experts
    blocks_per_sub = sub_experts // PEER_N_KEYS
    blocks_per_chunk = te // PEER_N_KEYS
    assert blocks_per_chunk % SUBLANES == 0

    @pl.when(c == 0)
    def _():
        acc_ref[...] = jnp.zeros_like(acc_ref)
        rho_ref[...] = rho_in_ref[...]
        bet_ref[...] = bet_in_ref[...]

    ht = ht_ref[...]
    row0 = pl.multiple_of(c * blocks_per_chunk, SUBLANES)

    def expert_act(s):
        return _dot(u_ref[pl.ds(s * sub_experts, sub_experts), :], ht)

    def gated(s, act):
        act = 0.5 * act * (1.0 + lax.erf(act * INV_SQRT2))
        actb = act.astype(BF16)
        parts = []
        for b in range(blocks_per_sub):
            r = s * blocks_per_sub + b
            tile0 = row0 + (r // SUBLANES) * SUBLANES
            rr = r % SUBLANES
            gate = None
            for hd in range(PEER_HEADS):
                kap8 = kap_ref[hd, pl.ds(tile0, SUBLANES), :]
                alp8 = alp_ref[hd, pl.ds(tile0, SUBLANES), :]
                kap = pltpu.bitcast(jnp.broadcast_to(kap8[rr:rr + 1, :], (PEER_N_KEYS // 2, tm)), BF16)
                alp = pltpu.bitcast(jnp.broadcast_to(alp8[rr:rr + 1, :], (PEER_N_KEYS // 2, tm)), BF16)
                term = jnp.where(rho_ref[hd] < kap, alp * bet_ref[hd], jnp.zeros((), BF16))
                gate = term if gate is None else gate + term
            parts.append(gate * actb[b * PEER_N_KEYS:(b + 1) * PEER_N_KEYS, :])
        return jnp.concatenate(parts, axis=0) if len(parts) > 1 else parts[0]

    acts = {s: expert_act(s) for s in range(min(lookahead, n_sub))}
    for s in range(n_sub):
        if s + lookahead < n_sub:
            acts[s + lookahead] = expert_act(s + lookahead)
        p = gated(s, acts.pop(s))
        acc_ref[...] += _dot(vt_ref[:, pl.ds(s * sub_experts, sub_experts)], p)

    @pl.when(c == pl.num_programs(1) - 1)
    def _():
        o_ref[...] = x_ref[...] + acc_ref[...].T


def peer_dense(x, ht, u, vt, kap, alp, rho, bet, *, tm=512, te=2048, sub_experts=256, lookahead=3):
    n_tok, dm = x.shape
    n_exp = u.shape[0]
    route_spec = pl.BlockSpec((PEER_HEADS, PEER_N_KEYS, tm), lambda t, c: (0, 0, t))
    return pl.pallas_call(
        functools.partial(_peer_dense_kernel, sub_experts=sub_experts, lookahead=lookahead),
        grid=(n_tok // tm, n_exp // te),
        in_specs=[
            pl.BlockSpec((tm, dm), lambda t, c: (t, 0)),
            pl.BlockSpec((dm, tm), lambda t, c: (0, t)),
            pl.BlockSpec((te, dm), lambda t, c: (c, 0)),
            pl.BlockSpec((dm, te), lambda t, c: (0, c)),
            route_spec, route_spec, route_spec, route_spec,
        ],
        out_specs=pl.BlockSpec((tm, dm), lambda t, c: (t, 0)),
        out_shape=jax.ShapeDtypeStruct(x.shape, x.dtype),
        scratch_shapes=[
            pltpu.VMEM((dm, tm), F32),
            pltpu.VMEM((PEER_HEADS, PEER_N_KEYS, tm), BF16),
            pltpu.VMEM((PEER_HEADS, PEER_N_KEYS, tm), BF16),
        ],
        input_output_aliases={0: 0},
        compiler_params=_cparams(("parallel", "arbitrary")),
        name="peer_dense",
    )(x, ht, u, vt, kap, alp, rho, bet)


def _ple_kernel(x_ref, p_ref, gain_ref, wg_ref, wp_ref, o_ref):
    x = x_ref[...]
    h = _rms(x, gain_ref[...]).astype(BF16)
    gate = jax.nn.sigmoid(_dot(h, wg_ref[...]))
    proj = _dot(p_ref[...].astype(BF16), wp_ref[...])
    o_ref[...] = x + gate * proj


def ple(x, p, gain, w_gate, w_proj, *, tm=512):
    n_tok, dm = x.shape
    dp = p.shape[1]
    return pl.pallas_call(
        _ple_kernel,
        grid=(n_tok // tm,),
        in_specs=[
            pl.BlockSpec((tm, dm), lambda t: (t, 0)),
            pl.BlockSpec((tm, dp), lambda t: (t, 0)),
            pl.BlockSpec((1, dm), lambda t: (0, 0)),
            pl.BlockSpec((dm, dm), lambda t: (0, 0)),
            pl.BlockSpec((dp, dm), lambda t: (0, 0)),
        ],
        out_specs=pl.BlockSpec((tm, dm), lambda t: (t, 0)),
        out_shape=jax.ShapeDtypeStruct(x.shape, x.dtype),
        input_output_aliases={0: 0},
        compiler_params=_cparams(("parallel",)),
        name="ple",
    )(x, p, gain, w_gate, w_proj)


def _head_norm_rope(y, gain, cosf, sinf, rot_mat):
    yn = _rms(y, gain)
    return yn * cosf + _dot(yn.astype(BF16), rot_mat) * sinf


def _store_residue_major(perm_ref, y, out_ref):
    d, m, w = out_ref.shape
    out_ref[...] = _dot(perm_ref[...], y).astype(out_ref.dtype).reshape(d, m, w)


def _kv_kernel(x_ref, gain_ref, w_ref, kgain_ref, cos_ref, sin_ref, rot_ref, perm1_ref, perm2_ref,
               k0_ref, v0_ref, k1_ref, v1_ref, k2_ref, v2_ref):
    dm = x_ref.shape[1]
    h = _rms(x_ref[...], gain_ref[...]).astype(BF16)
    kv = _dot(h, w_ref[...])
    v0_ref[...] = kv[:, dm:].astype(BF16)
    cosf, sinf, kg, rot = cos_ref[...], sin_ref[...], kgain_ref[...], rot_ref[...]
    for hd in range(KV_HEADS):
        sl = slice(hd * HEAD_DIM, (hd + 1) * HEAD_DIM)
        k0_ref[:, sl] = _head_norm_rope(kv[:, sl], kg, cosf, sinf, rot).astype(BF16)
    for perm_ref, k_ref, v_ref in ((perm1_ref, k1_ref, v1_ref), (perm2_ref, k2_ref, v2_ref)):
        _store_residue_major(perm_ref, k0_ref[...], k_ref)
        _store_residue_major(perm_ref, v0_ref[...], v_ref)


def _residue_major_spec(tm, dilation, width, tiles_per_seq):
    return pl.BlockSpec((None, dilation, None, tm // dilation, width),
                        lambda t: (t // tiles_per_seq, 0, t % tiles_per_seq, 0, 0))


def _residue_major_shape(batch, seq, tm, dilation, width, dtype):
    return jax.ShapeDtypeStruct((batch, dilation, seq // tm, tm // dilation, width), dtype)


def shared_kv(x, gain, w_kv, k_gain, cosf, sinf, rot_mat, perms, *, batch, seq, tm=512):
    n_tok, dm = x.shape
    tps = seq // tm
    row = pl.BlockSpec((tm, dm), lambda t: (t, 0))
    dils = [d for _, d in DILATION_GROUPS[1:]]
    const2 = lambda t: (0, 0)
    outs = pl.pallas_call(
        _kv_kernel,
        grid=(n_tok // tm,),
        in_specs=[
            row,
            pl.BlockSpec((1, dm), const2),
            pl.BlockSpec((dm, 2 * dm), const2),
            pl.BlockSpec((1, HEAD_DIM), const2),
            pl.BlockSpec((tm, HEAD_DIM), lambda t: (t, 0)),
            pl.BlockSpec((tm, HEAD_DIM), lambda t: (t, 0)),
            pl.BlockSpec((HEAD_DIM, HEAD_DIM), const2),
            pl.BlockSpec((tm, tm), const2),
            pl.BlockSpec((tm, tm), const2),
        ],
        out_specs=[row, row] + [_residue_major_spec(tm, d, dm, tps) for d in dils for _ in range(2)],
        out_shape=[jax.ShapeDtypeStruct((n_tok, dm), BF16)] * 2
        + [_residue_major_shape(batch, seq, tm, d, dm, BF16) for d in dils for _ in range(2)],
        compiler_params=_cparams(("parallel",)),
        name="shared_kv",
    )(x, gain, w_kv, k_gain, cosf, sinf, rot_mat, *perms)
    return [(outs[0], outs[1]), (outs[2], outs[3]), (outs[4], outs[5])]


def _q_kernel(x_ref, gain_ref, w_ref, qgain_ref, cos_ref, sin_ref, rot_ref, perm1_ref, perm2_ref,
              q0_ref, q1_ref, q2_ref, qbuf_ref):
    dm = x_ref.shape[1]
    h = _rms(x_ref[...], gain_ref[...]).astype(BF16)
    q = _dot(h, w_ref[...])
    cosf, sinf, qg, rot = cos_ref[...], sin_ref[...], qgain_ref[...], rot_ref[...]
    for hd in range(q.shape[1] // HEAD_DIM):
        sl = slice(hd * HEAD_DIM, (hd + 1) * HEAD_DIM)
        y = _head_norm_rope(q[:, sl], qg, cosf, sinf, rot).astype(BF16)
        if hd < KV_HEADS:
            q0_ref[:, sl] = y
        else:
            qbuf_ref[:, hd * HEAD_DIM - dm:(hd + 1) * HEAD_DIM - dm] = y
    _store_residue_major(perm1_ref, qbuf_ref[:, :dm], q1_ref)
    _store_residue_major(perm2_ref, qbuf_ref[:, dm:], q2_ref)


def query_proj(x, gain, w_q, q_gain, cosf, sinf, rot_mat, perms, *, batch, seq, tm=512):
    n_tok, dm = x.shape
    nq = w_q.shape[1]
    tps = seq // tm
    dils = [d for _, d in DILATION_GROUPS[1:]]
    const2 = lambda t: (0, 0)
    return pl.pallas_call(
        _q_kernel,
        grid=(n_tok // tm,),
        in_specs=[
            pl.BlockSpec((tm, dm), lambda t: (t, 0)),
            pl.BlockSpec((1, dm), const2),
            pl.BlockSpec((dm, nq), const2),
            pl.BlockSpec((1, HEAD_DIM), const2),
            pl.BlockSpec((tm, HEAD_DIM), lambda t: (t, 0)),
            pl.BlockSpec((tm, HEAD_DIM), lambda t: (t, 0)),
            pl.BlockSpec((HEAD_DIM, HEAD_DIM), const2),
            pl.BlockSpec((tm, tm), const2),
            pl.BlockSpec((tm, tm), const2),
        ],
        out_specs=[pl.BlockSpec((tm, dm), lambda t: (t, 0))] + [_residue_major_spec(tm, d, dm, tps) for d in dils],
        out_shape=[jax.ShapeDtypeStruct((n_tok, dm), BF16)]
        + [_residue_major_shape(batch, seq, tm, d, dm, BF16) for d in dils],
        scratch_shapes=[pltpu.VMEM((tm, nq - dm), BF16)],
        compiler_params=_cparams(("parallel",)),
        name="query_proj",
    )(x, gain, w_q, q_gain, cosf, sinf, rot_mat, *perms)


ATTN_BLOCK = 128


def _attn_scores(q, k):
    return lax.dot_general(q, k, (((1,), (1,)), ((), ())), preferred_element_type=F32) * (HEAD_DIM ** -0.5)


def _attn_softmax(s, mask):
    s = jnp.where(mask, s, NEG_INF)
    m = jnp.max(s, axis=-1, keepdims=True)
    e = jnp.exp(s - m)
    den = jnp.sum(e, axis=-1, keepdims=True)
    return (e / den).astype(BF16), m + jnp.log(den)


def _attn_kernel(q_ref, k_ref, v_ref, o_ref, lse_ref):
    length = q_ref.shape[0]
    blk = ATTN_BLOCK
    n_blk = length // blk
    qi = lax.broadcasted_iota(jnp.int32, (blk, blk), 0)
    ki = lax.broadcasted_iota(jnp.int32, (blk, blk), 1)
    mask_first = ki <= qi
    qi2 = lax.broadcasted_iota(jnp.int32, (blk, 2 * blk), 0)
    ki2 = lax.broadcasted_iota(jnp.int32, (blk, 2 * blk), 1)
    dist = qi2 + blk - ki2
    mask_band = (dist >= 0) & (dist <= blk)
    lane = lax.broadcasted_iota(jnp.int32, (blk, LANES), 1)

    def one_block(q_rows, k_rows, mask):
        heads = [slice(hd * HEAD_DIM, (hd + 1) * HEAD_DIM) for hd in range(KV_HEADS)]
        scores = [_attn_scores(q_ref[q_rows, sl], k_ref[k_rows, sl]) for sl in heads]
        probs = [_attn_softmax(s, mask) for s in scores]
        lse_tile = jnp.zeros((blk, LANES), F32)
        for hd, sl in enumerate(heads):
            p, lse = probs[hd]
            o_ref[q_rows, sl] = _dot(p, v_ref[k_rows, sl]).astype(o_ref.dtype)
            lse_tile = jnp.where(lane == hd, lse, lse_tile)
        lse_ref[q_rows, :] = lse_tile

    one_block(pl.ds(0, blk), pl.ds(0, blk), mask_first)

    def body(n, carry):
        q0 = pl.multiple_of(n * blk, blk)
        k0 = pl.multiple_of((n - 1) * blk, blk)
        one_block(pl.ds(q0, blk), pl.ds(k0, 2 * blk), mask_band)
        return carry

    lax.fori_loop(1, n_blk, body, 0)


def dilated_attention(q, k, v, *, batch, seq, dilation):
    width = k.shape[-1]
    length = seq // dilation
    spec = pl.BlockSpec((None, None, length, width), lambda b, r: (b, r, 0, 0))
    return pl.pallas_call(
        _attn_kernel,
        grid=(batch, dilation),
        in_specs=[spec, spec, spec],
        out_specs=[spec, pl.BlockSpec((None, None, length, LANES), lambda b, r: (b, r, 0, 0))],
        out_shape=[
            jax.ShapeDtypeStruct((batch, dilation, length, width), BF16),
            jax.ShapeDtypeStruct((batch, dilation, length, LANES), F32),
        ],
        compiler_params=_cparams(("parallel", "parallel")),
        name=f"dilated_attention_d{dilation}",
    )(q, k, v)


def _split3_bf16(v):
    hi = v.astype(BF16)
    r1 = v - hi.astype(F32)
    mid = r1.astype(BF16)
    lo = (r1 - mid.astype(F32)).astype(BF16)
    return hi, mid, lo


def _attn_out_kernel(x_ref, o0_ref, o1_ref, o2_ref, l0_ref, l1_ref, l2_ref, unperm1_ref, unperm2_ref,
                     expand_ref, wo_ref, out_ref):
    tm, dm = x_ref.shape
    outs = [o0_ref[...].astype(F32)]
    lses = [l0_ref[...]]
    for o_ref, l_ref, unperm_ref in ((o1_ref, l1_ref, unperm1_ref), (o2_ref, l2_ref, unperm2_ref)):
        unperm = unperm_ref[...]
        outs.append(_dot(unperm, o_ref[...].reshape(tm, dm)))
        pieces = _split3_bf16(l_ref[...].reshape(tm, LANES))
        lses.append(_dot(unperm, pieces[0]) + _dot(unperm, pieces[1]) + _dot(unperm, pieces[2]))
    m = jnp.maximum(jnp.maximum(lses[0], lses[1]), lses[2])
    es = [jnp.exp(l - m) for l in lses]
    inv = 1.0 / (es[0] + es[1] + es[2])
    expand = expand_ref[...]
    mix = None
    for e, o in zip(es, outs):
        w = e * inv
        w_hi = w.astype(BF16)
        w_lo = (w - w_hi.astype(F32)).astype(BF16)
        w_full = _dot(w_hi, expand) + _dot(w_lo, expand)
        term = w_full * o
        mix = term if mix is None else mix + term
    out_ref[...] = x_ref[...] + _dot(mix.astype(BF16), wo_ref[...])


def attention_out(x, outs, lses, unperms, expand, w_o, *, batch, seq, tm=512):
    n_tok, dm = x.shape
    tps = seq // tm
    dils = [d for _, d in DILATION_GROUPS]
    row = lambda t: (t, 0)
    const = lambda t: (0, 0)
    o_specs = [pl.BlockSpec((tm, dm), row)] + [_residue_major_spec(tm, d, dm, tps) for d in dils[1:]]
    l_specs = [pl.BlockSpec((tm, LANES), row)] + [_residue_major_spec(tm, d, LANES, tps) for d in dils[1:]]
    o_args = [outs[0].reshape(n_tok, dm)] + [o.reshape(batch, d, tps, tm // d, dm) for o, d in zip(outs[1:], dils[1:])]
    l_args = [lses[0].reshape(n_tok, LANES)] + [l.reshape(batch, d, tps, tm // d, LANES)
                                                for l, d in zip(lses[1:], dils[1:])]
    return pl.pallas_call(
        _attn_out_kernel,
        grid=(n_tok // tm,),
        in_specs=[pl.BlockSpec((tm, dm), row)] + o_specs + l_specs
        + [pl.BlockSpec((tm, tm), const)] * 2 + [pl.BlockSpec((LANES, dm), const), pl.BlockSpec((dm, dm), const)],
        out_specs=pl.BlockSpec((tm, dm), row),
        out_shape=jax.ShapeDtypeStruct(x.shape, x.dtype),
        input_output_aliases={0: 0},
        compiler_params=_cparams(("parallel",)),
        name="attention_out",
    )(x, *o_args, *l_args, *unperms, expand, w_o)


def _rope_tables(positions):
    half = ROPE_DIMS // 2
    inv_freq = 1.0 / (ROPE_THETA ** (jnp.arange(0, ROPE_DIMS, 2, dtype=F32) / ROPE_DIMS))
    ang = positions.astype(F32).reshape(-1, 1) * inv_freq
    cos, sin = jnp.cos(ang), jnp.sin(ang)
    pad = HEAD_DIM - ROPE_DIMS
    cosf = jnp.concatenate([cos, cos, jnp.ones((cos.shape[0], pad), F32)], axis=-1)
    sinf = jnp.concatenate([-sin, sin, jnp.zeros((sin.shape[0], pad), F32)], axis=-1)
    assert half * 2 == ROPE_DIMS
    return cosf, sinf


def _residue_perm(tm, dilation):
    m = tm // dilation
    idx = jnp.arange(tm)
    src_row = (idx % m) * dilation + idx // m
    return (src_row[:, None] == jnp.arange(tm)[None, :]).astype(BF16)


def kernel(x, p, positions, a_norm, a_w_in, a_conv, a_w_out, kv_norm, w_kv, k_norm, b_norm, b_w_q, q_norm, b_w_o, ffn_norm, peer_w_q, peer_sub_keys, peer_u, peer_v, ple_norm, ple_w_gate, ple_w_proj):
    batch, seq, dm = x.shape
    depth = p.shape[0]
    n_a = a_norm.shape[0]
    n_tok = batch * seq
    tm = 512
    xf = x.reshape(n_tok, dm)
    cosf, sinf = _rope_tables(positions)
    head_of_lane = jnp.arange(dm) // HEAD_DIM
    expand = (jnp.arange(LANES)[:, None] == head_of_lane[None, :]).astype(BF16)
    dils = [d for _, d in DILATION_GROUPS]
    assert dils[0] == 1 and all(w // d == ATTN_BLOCK for w, d in DILATION_GROUPS)
    perms = [_residue_perm(tm, d) for d in dils[1:]]
    half = ROPE_DIMS // 2
    lane = jnp.arange(HEAD_DIM)
    src_lane = jnp.where(lane < half, lane + half, lane - half)
    rot_mat = ((jnp.arange(HEAD_DIM)[:, None] == src_lane[None, :]) & (lane < ROPE_DIMS)[None, :]).astype(BF16)
    unperms = [pm.T for pm in perms]

    def residue_major(a, d):
        return a.reshape(batch, d, seq // d, a.shape[-1])

    kvs = None
    for i in range(depth):
        if i < n_a:
            xf = conv_mixer(xf, a_norm[i][None], a_w_in[i].astype(BF16), a_conv[i], a_w_out[i].astype(BF16), seq=seq)
        else:
            j = i - n_a
            qs = query_proj(xf, b_norm[j][None], b_w_q[j].astype(BF16), q_norm[j][None], cosf, sinf, rot_mat, perms,
                            batch=batch, seq=seq, tm=tm)
            outs, lses = [], []
            for q_g, (k_g, v_g), d in zip(qs, kvs, dils):
                o_g, lse_g = dilated_attention(residue_major(q_g, d), residue_major(k_g, d), residue_major(v_g, d),
                                               batch=batch, seq=seq, dilation=d)
                outs.append(o_g)
                lses.append(lse_g)
            xf = attention_out(xf, outs, lses, unperms, expand, b_w_o[j].astype(BF16), batch=batch, seq=seq, tm=tm)
        ht, kap, alp, rho, bet = peer_route(xf, ffn_norm[i][None], peer_w_q[i].T.astype(BF16),
                                            peer_sub_keys[i].astype(BF16))
        xf = peer_dense(xf, ht, peer_u[i].astype(BF16), peer_v[i].T.astype(BF16), kap, alp, rho, bet)
        xf = ple(xf, p[i].reshape(n_tok, -1), ple_norm[i][None], ple_w_gate[i].astype(BF16),
                 ple_w_proj[i].astype(BF16))
        if i == n_a - 1:
            kvs = shared_kv(xf, kv_norm[None], w_kv.astype(BF16), k_norm[None], cosf, sinf, rot_mat, perms,
                            batch=batch, seq=seq, tm=tm)
    return xf.reshape(batch, seq, dm)
```

```python
import functools

import jax
import jax.numpy as jnp
from jax import lax
from jax.experimental import pallas as pl
from jax.experimental.pallas import tpu as pltpu

F32 = jnp.float32
BF16 = jnp.bfloat16
U32 = jnp.uint32

LANES = 128
SUBLANES = 8
VMEM_LIMIT_BYTES = 56 * 1024 * 1024

NORM_EPS = 1e-6
HEAD_DIM = 128
KV_HEADS = 8
N_GROUPS = 3
ROPE_DIMS = HEAD_DIM // 4
ROPE_THETA = 500000.0
DILATION_GROUPS = ((128, 1), (512, 4), (2048, 16))
PEER_HEADS = 8
PEER_N_KEYS = 128
PEER_TOPK = 16
INV_SQRT2 = 0.7071067811865476
NEG_INF = float("-inf")


def _cparams(sem):
    return pltpu.CompilerParams(dimension_semantics=sem, vmem_limit_bytes=VMEM_LIMIT_BYTES)


def _rms(x, gain):
    ms = jnp.mean(x * x, axis=-1, keepdims=True)
    return x * lax.rsqrt(ms + NORM_EPS) * gain


def _dot(a, b):
    return jnp.dot(a, b, preferred_element_type=F32)


def _conv_mixer_kernel(x_ref, gain_ref, win_ref, conv_ref, wout_ref, o_ref, gbuf_ref):
    j = pl.program_id(1)
    tm, dm = x_ref.shape
    x = x_ref[...]
    h = _rms(x, gain_ref[...]).astype(BF16)
    bcu = _dot(h, win_ref[...])
    b_gate = bcu[:, :dm]
    g = bcu[:, dm:2 * dm] * bcu[:, 2 * dm:]

    @pl.when(j == 0)
    def _():
        gbuf_ref[0:SUBLANES, :] = jnp.zeros((SUBLANES, dm), F32)

    gbuf_ref[SUBLANES:SUBLANES + tm, :] = g
    g1 = gbuf_ref[SUBLANES - 1:SUBLANES - 1 + tm, :]
    g2 = gbuf_ref[SUBLANES - 2:SUBLANES - 2 + tm, :]
    w = conv_ref[...]
    z = w[0:1, :] * g2 + w[1:2, :] * g1 + w[2:3, :] * g
    gbuf_ref[0:SUBLANES, :] = g[tm - SUBLANES:, :]
    y = _dot((b_gate * z).astype(BF16), wout_ref[...])
    o_ref[...] = x + y


def conv_mixer(x, gain, w_in, conv_w, w_out, *, seq, tm=512):
    n_tok, dm = x.shape
    tiles_per_seq = seq // tm
    row = lambda b, j: (b * tiles_per_seq + j, 0)
    const = lambda b, j: (0, 0)
    return pl.pallas_call(
        _conv_mixer_kernel,
        grid=(n_tok // seq, tiles_per_seq),
        in_specs=[
            pl.BlockSpec((tm, dm), row),
            pl.BlockSpec((1, dm), const),
            pl.BlockSpec((dm, 3 * dm), const),
            pl.BlockSpec(conv_w.shape, const),
            pl.BlockSpec((dm, dm), const),
        ],
        out_specs=pl.BlockSpec((tm, dm), row),
        out_shape=jax.ShapeDtypeStruct(x.shape, x.dtype),
        scratch_shapes=[pltpu.VMEM((tm + SUBLANES, dm), F32)],
        input_output_aliases={0: 0},
        compiler_params=_cparams(("arbitrary", "arbitrary")),
        name="conv_mixer",
    )(x, gain, w_in, conv_w, w_out)


def _staircase():
    return [(k1, k2) for k1 in range(PEER_TOPK) for k2 in range(PEER_TOPK) if (k1 + 1) * (k2 + 1) <= PEER_TOPK]


def _tree_max(vals):
    vals = list(vals)
    while len(vals) > 1:
        nxt = [jnp.maximum(vals[a], vals[a + 1]) for a in range(0, len(vals) - 1, 2)]
        if len(vals) % 2:
            nxt.append(vals[-1])
        vals = nxt
    return vals[0]


def _dup_bf16_bits(v):
    bits = lax.bitcast_convert_type(v, U32)
    return bits | (bits >> 16)


def _sorting_network(n):
    pairs = []
    p = 1
    while p < n:
        k = p
        while k >= 1:
            for j in range(k % p, n - k, 2 * k):
                for i in range(min(k, n - j - k)):
                    if (i + j) // (2 * p) == (i + j + k) // (2 * p):
                        pairs.append((i + j, i + j + k))
            k //= 2
        p *= 2
    return pairs


def _compare_exchange(xs, i, j):
    xs[i], xs[j] = jnp.maximum(xs[i], xs[j]), jnp.minimum(xs[i], xs[j])


def _top_sorted(s3):
    n = PEER_TOPK
    assert s3.shape[0] == n and s3.shape[1] == SUBLANES
    xs = [s3[v] for v in range(n)]
    for i, j in _sorting_network(n):
        _compare_exchange(xs, i, j)
    shift = SUBLANES // 2
    while shift >= 1:
        ys = [pltpu.roll(x, shift, 0) for x in xs]
        xs = [jnp.maximum(xs[k], ys[n - 1 - k]) for k in range(n)]
        d = n // 2
        while d >= 1:
            for k in range(n):
                if k & d == 0:
                    _compare_exchange(xs, k, k + d)
            d //= 2
        shift //= 2
    return xs


def _peer_route_kernel(x_ref, gain_ref, wqt_ref, keys_ref,
                       ht_ref, kap_ref, alp_ref, rho_ref, bet_ref, s1_ref, s2_ref):
    tr = x_ref.shape[0]
    groups = PEER_N_KEYS // SUBLANES
    h = _rms(x_ref[...], gain_ref[...])
    ht = h.T.astype(BF16)
    ht_ref[...] = ht
    qt = _dot(wqt_ref[...], ht).astype(BF16)

    tops = {}
    for hd in range(PEER_HEADS):
        for half in range(2):
            r0 = (hd * 2 + half) * PEER_N_KEYS
            s = _dot(keys_ref[hd, half], qt[r0:r0 + PEER_N_KEYS, :])
            (s1_ref if half == 0 else s2_ref)[hd] = s
            tops[hd, half] = _top_sorted(s.reshape(groups, SUBLANES, tr))

    a_k = [jnp.concatenate([tops[hd, 0][k][0:1, :] for hd in range(PEER_HEADS)], axis=0) for k in range(PEER_TOPK)]
    b_k = [jnp.concatenate([tops[hd, 1][k][0:1, :] for hd in range(PEER_HEADS)], axis=0) for k in range(PEER_TOPK)]
    stairs = _staircase()
    cands = [a_k[k1] + b_k[k2] for k1, k2 in stairs]
    work = cands
    for _ in range(PEER_TOPK - 1):
        m = _tree_max(work)
        work = [jnp.where(c == m, NEG_INF, c) for c in work]
    tau = _tree_max(work)
    top_sum = cands[0]
    z = None
    theta = [None] * PEER_TOPK
    for (k1, k2), c in zip(stairs, cands):
        sel = c >= tau
        e = jnp.where(sel, jnp.exp(c - top_sum), 0.0)
        z = e if z is None else z + e
        t = jnp.where(sel, a_k[k1], jnp.inf)
        theta[k2] = t if theta[k2] is None else jnp.minimum(theta[k2], t)
    inv_z = 0.5 / z

    for hd in range(PEER_HEADS):
        s1 = s1_ref[hd].reshape(groups, SUBLANES, tr)
        s2 = s2_ref[hd].reshape(groups, SUBLANES, tr)
        row = lambda v: jnp.broadcast_to(v[hd:hd + 1, :], (SUBLANES, tr))[None]
        kap = jnp.zeros(s1.shape, F32)
        for k2 in range(PEER_TOPK):
            kap = jnp.where(s1 >= row(theta[k2]), float(k2 + 1), kap)
        rho = jnp.full(s2.shape, float(PEER_TOPK), F32)
        for k in reversed(range(PEER_TOPK)):
            rho = jnp.where(s2 >= tops[hd, 1][k][None], float(k), rho)
        alp = jnp.exp(s1 - tops[hd, 0][0][None]).astype(BF16).astype(F32)
        bet = jnp.exp(s2 - tops[hd, 1][0][None]) * row(inv_z)
        kap_ref[hd] = _dup_bf16_bits(kap).reshape(PEER_N_KEYS, tr)
        alp_ref[hd] = _dup_bf16_bits(alp).reshape(PEER_N_KEYS, tr)
        rho_ref[hd] = rho.reshape(PEER_N_KEYS, tr).astype(BF16)
        bet_ref[hd] = bet.reshape(PEER_N_KEYS, tr).astype(BF16)


def peer_route(x, gain, wq_t, sub_keys, *, tr=256):
    n_tok, dm = x.shape
    nq = wq_t.shape[0]
    route_spec = pl.BlockSpec((PEER_HEADS, PEER_N_KEYS, tr), lambda t: (0, 0, t))
    route_shape = (PEER_HEADS, PEER_N_KEYS, n_tok)
    return pl.pallas_call(
        _peer_route_kernel,
        grid=(n_tok // tr,),
        in_specs=[
            pl.BlockSpec((tr, dm), lambda t: (t, 0)),
            pl.BlockSpec((1, dm), lambda t: (0, 0)),
            pl.BlockSpec((nq, dm), lambda t: (0, 0)),
            pl.BlockSpec(sub_keys.shape, lambda t: (0, 0, 0, 0)),
        ],
        out_specs=[pl.BlockSpec((dm, tr), lambda t: (0, t)), route_spec, route_spec, route_spec, route_spec],
        out_shape=[
            jax.ShapeDtypeStruct((dm, n_tok), BF16),
            jax.ShapeDtypeStruct(route_shape, U32),
            jax.ShapeDtypeStruct(route_shape, U32),
            jax.ShapeDtypeStruct(route_shape, BF16),
            jax.ShapeDtypeStruct(route_shape, BF16),
        ],
        scratch_shapes=[pltpu.VMEM((PEER_HEADS, PEER_N_KEYS, tr), F32)] * 2,
        compiler_params=_cparams(("parallel",)),
        name="peer_route",
    )(x, gain, wq_t, sub_keys)


def _ple(x, p, gain, w_gate, w_proj):
    h = _rms(x, gain).astype(BF16)
    return x + jax.nn.sigmoid(_dot(h, w_gate)) * _dot(p.astype(BF16), w_proj)


def _peer_dense_kernel(x_ref, ht_ref, u_ref, vt_ref, kap_ref, alp_ref, rho_in_ref, bet_in_ref,
                       p_ref, ple_gain_ref, wg_ref, wp_ref,
                       o_ref, acc_ref, rho_ref, bet_ref, *, sub_experts, lookahead):
    c = pl.program_id(1)
    te = u_ref.shape[0]
    tm = ht_ref.shape[1]
    n_sub = te // sub_experts
    blocks_per_sub = sub_experts // PEER_N_KEYS
    blocks_per_chunk = te // PEER_N_KEYS
    assert blocks_per_chunk % SUBLANES == 0

    @pl.when(c == 0)
    def _():
        acc_ref[...] = jnp.zeros_like(acc_ref)
        rho_ref[...] = rho_in_ref[...]
        bet_ref[...] = bet_in_ref[...]

    ht = ht_ref[...]
    row0 = pl.multiple_of(c * blocks_per_chunk, SUBLANES)

    def expert_act(s):
        return _dot(u_ref[pl.ds(s * sub_experts, sub_experts), :], ht)

    def gated(s, act):
        act = act * (1.0 + lax.erf(act * INV_SQRT2))
        actb = act.astype(BF16)
        parts = []
        for b in range(blocks_per_sub):
            r = s * blocks_per_sub + b
            tile0 = row0 + (r // SUBLANES) * SUBLANES
            rr = r % SUBLANES
            gate = None
            for hd in range(PEER_HEADS):
                kap8 = kap_ref[hd, pl.ds(tile0, SUBLANES), :]
                alp8 = alp_ref[hd, pl.ds(tile0, SUBLANES), :]
                kap = pltpu.bitcast(jnp.broadcast_to(kap8[rr:rr + 1, :], (PEER_N_KEYS // 2, tm)), BF16)
                alp = pltpu.bitcast(jnp.broadcast_to(alp8[rr:rr + 1, :], (PEER_N_KEYS // 2, tm)), BF16)
                term = jnp.where(rho_ref[hd] < kap, alp * bet_ref[hd], jnp.zeros((), BF16))
                gate = term if gate is None else gate + term
            parts.append(gate * actb[b * PEER_N_KEYS:(b + 1) * PEER_N_KEYS, :])
        return jnp.concatenate(parts, axis=0) if len(parts) > 1 else parts[0]

    acts = {s: expert_act(s) for s in range(min(lookahead, n_sub))}
    for s in range(n_sub):
        if s + lookahead < n_sub:
            acts[s + lookahead] = expert_act(s + lookahead)
        p = gated(s, acts.pop(s))
        acc_ref[...] += _dot(vt_ref[:, pl.ds(s * sub_experts, sub_experts)], p)

    @pl.when(c == pl.num_programs(1) - 1)
    def _():
        x = x_ref[...] + acc_ref[...].T
        o_ref[...] = _ple(x, p_ref[...], ple_gain_ref[...], wg_ref[...], wp_ref[...])


def peer_dense(x, ht, u, vt, kap, alp, rho, bet, p, ple_gain, w_gate, w_proj, *,
               tm=512, te=2048, sub_experts=512, lookahead=3):
    n_tok, dm = x.shape
    n_exp = u.shape[0]
    dp = p.shape[1]
    route_spec = pl.BlockSpec((PEER_HEADS, PEER_N_KEYS, tm), lambda t, c: (0, 0, t))
    const = lambda t, c: (0, 0)
    return pl.pallas_call(
        functools.partial(_peer_dense_kernel, sub_experts=sub_experts, lookahead=lookahead),
        grid=(n_tok // tm, n_exp // te),
        in_specs=[
            pl.BlockSpec((tm, dm), lambda t, c: (t, 0)),
            pl.BlockSpec((dm, tm), lambda t, c: (0, t)),
            pl.BlockSpec((te, dm), lambda t, c: (c, 0)),
            pl.BlockSpec((dm, te), lambda t, c: (0, c)),
            route_spec, route_spec, route_spec, route_spec,
            pl.BlockSpec((tm, dp), lambda t, c: (t, 0)),
            pl.BlockSpec((1, dm), const),
            pl.BlockSpec((dm, dm), const),
            pl.BlockSpec((dp, dm), const),
        ],
        out_specs=pl.BlockSpec((tm, dm), lambda t, c: (t, 0)),
        out_shape=jax.ShapeDtypeStruct(x.shape, x.dtype),
        scratch_shapes=[
            pltpu.VMEM((dm, tm), F32),
            pltpu.VMEM((PEER_HEADS, PEER_N_KEYS, tm), BF16),
            pltpu.VMEM((PEER_HEADS, PEER_N_KEYS, tm), BF16),
        ],
        input_output_aliases={0: 0},
        compiler_params=_cparams(("parallel", "arbitrary")),
        name="peer_dense",
    )(x, ht, u, vt, kap, alp, rho, bet, p, ple_gain, w_gate, w_proj)


def _head_norm_rope(y, gain, cosf, sinf, rot_mat):
    yn = _rms(y, gain)
    return yn * cosf + _dot(yn.astype(BF16), rot_mat) * sinf


def _store_residue_major(perm_ref, y, out_ref):
    d, m, w = out_ref.shape
    out_ref[...] = _dot(perm_ref[...], y).astype(out_ref.dtype).reshape(d, m, w)


def _kv_kernel(x_ref, gain_ref, w_ref, kgain_ref, cos_ref, sin_ref, rot_ref, perm1_ref, perm2_ref,
               k0_ref, v0_ref, k1_ref, v1_ref, k2_ref, v2_ref):
    dm = x_ref.shape[1]
    h = _rms(x_ref[...], gain_ref[...]).astype(BF16)
    kv = _dot(h, w_ref[...])
    v0_ref[...] = kv[:, dm:].astype(BF16)
    cosf, sinf, kg, rot = cos_ref[...], sin_ref[...], kgain_ref[...], rot_ref[...]
    for hd in range(KV_HEADS):
        sl = slice(hd * HEAD_DIM, (hd + 1) * HEAD_DIM)
        k0_ref[:, sl] = _head_norm_rope(kv[:, sl], kg, cosf, sinf, rot).astype(BF16)
    for perm_ref, k_ref, v_ref in ((perm1_ref, k1_ref, v1_ref), (perm2_ref, k2_ref, v2_ref)):
        _store_residue_major(perm_ref, k0_ref[...], k_ref)
        _store_residue_major(perm_ref, v0_ref[...], v_ref)


def _residue_major_spec(tm, dilation, width, tiles_per_seq):
    return pl.BlockSpec((None, dilation, None, tm // dilation, width),
                        lambda t: (t // tiles_per_seq, 0, t % tiles_per_seq, 0, 0))


def _residue_major_shape(batch, seq, tm, dilation, width, dtype):
    return jax.ShapeDtypeStruct((batch, dilation, seq // tm, tm // dilation, width), dtype)


def shared_kv(x, gain, w_kv, k_gain, cosf, sinf, rot_mat, perms, *, batch, seq, tm=512):
    n_tok, dm = x.shape
    tps = seq // tm
    row = pl.BlockSpec((tm, dm), lambda t: (t, 0))
    dils = [d for _, d in DILATION_GROUPS[1:]]
    const2 = lambda t: (0, 0)
    outs = pl.pallas_call(
        _kv_kernel,
        grid=(n_tok // tm,),
        in_specs=[
            row,
            pl.BlockSpec((1, dm), const2),
            pl.BlockSpec((dm, 2 * dm), const2),
            pl.BlockSpec((1, HEAD_DIM), const2),
            pl.BlockSpec((tm, HEAD_DIM), lambda t: (t, 0)),
            pl.BlockSpec((tm, HEAD_DIM), lambda t: (t, 0)),
            pl.BlockSpec((HEAD_DIM, HEAD_DIM), const2),
            pl.BlockSpec((tm, tm), const2),
            pl.BlockSpec((tm, tm), const2),
        ],
        out_specs=[row, row] + [_residue_major_spec(tm, d, dm, tps) for d in dils for _ in range(2)],
        out_shape=[jax.ShapeDtypeStruct((n_tok, dm), BF16)] * 2
        + [_residue_major_shape(batch, seq, tm, d, dm, BF16) for d in dils for _ in range(2)],
        compiler_params=_cparams(("parallel",)),
        name="shared_kv",
    )(x, gain, w_kv, k_gain, cosf, sinf, rot_mat, *perms)
    return [(outs[0], outs[1]), (outs[2], outs[3]), (outs[4], outs[5])]


def _q_kernel(x_ref, gain_ref, w_ref, qgain_ref, cos_ref, sin_ref, rot_ref, perm1_ref, perm2_ref,
              q0_ref, q1_ref, q2_ref, qbuf_ref):
    dm = x_ref.shape[1]
    h = _rms(x_ref[...], gain_ref[...]).astype(BF16)
    q = _dot(h, w_ref[...])
    cosf, sinf, qg, rot = cos_ref[...], sin_ref[...], qgain_ref[...], rot_ref[...]
    for hd in range(q.shape[1] // HEAD_DIM):
        sl = slice(hd * HEAD_DIM, (hd + 1) * HEAD_DIM)
        y = _head_norm_rope(q[:, sl], qg, cosf, sinf, rot).astype(BF16)
        if hd < KV_HEADS:
            q0_ref[:, sl] = y
        else:
            qbuf_ref[:, hd * HEAD_DIM - dm:(hd + 1) * HEAD_DIM - dm] = y
    _store_residue_major(perm1_ref, qbuf_ref[:, :dm], q1_ref)
    _store_residue_major(perm2_ref, qbuf_ref[:, dm:], q2_ref)


def query_proj(x, gain, w_q, q_gain, cosf, sinf, rot_mat, perms, *, batch, seq, tm=512):
    n_tok, dm = x.shape
    nq = w_q.shape[1]
    tps = seq // tm
    dils = [d for _, d in DILATION_GROUPS[1:]]
    const2 = lambda t: (0, 0)
    return pl.pallas_call(
        _q_kernel,
        grid=(n_tok // tm,),
        in_specs=[
            pl.BlockSpec((tm, dm), lambda t: (t, 0)),
            pl.BlockSpec((1, dm), const2),
            pl.BlockSpec((dm, nq), const2),
            pl.BlockSpec((1, HEAD_DIM), const2),
            pl.BlockSpec((tm, HEAD_DIM), lambda t: (t, 0)),
            pl.BlockSpec((tm, HEAD_DIM), lambda t: (t, 0)),
            pl.BlockSpec((HEAD_DIM, HEAD_DIM), const2),
            pl.BlockSpec((tm, tm), const2),
            pl.BlockSpec((tm, tm), const2),
        ],
        out_specs=[pl.BlockSpec((tm, dm), lambda t: (t, 0))] + [_residue_major_spec(tm, d, dm, tps) for d in dils],
        out_shape=[jax.ShapeDtypeStruct((n_tok, dm), BF16)]
        + [_residue_major_shape(batch, seq, tm, d, dm, BF16) for d in dils],
        scratch_shapes=[pltpu.VMEM((tm, nq - dm), BF16)],
        compiler_params=_cparams(("parallel",)),
        name="query_proj",
    )(x, gain, w_q, q_gain, cosf, sinf, rot_mat, *perms)


ATTN_BLOCK = 128


def _attn_scores(q, k):
    return lax.dot_general(q, k, (((1,), (1,)), ((), ())), preferred_element_type=F32) * (HEAD_DIM ** -0.5)


def _attn_softmax(s, mask):
    s = jnp.where(mask, s, NEG_INF)
    m = jnp.max(s, axis=-1, keepdims=True)
    e = jnp.exp(s - m)
    den = jnp.sum(e, axis=-1, keepdims=True)
    return (e / den).astype(BF16), m + jnp.log(den)


def _attn_kernel(q_ref, k_ref, v_ref, o_ref, lse_ref):
    length = q_ref.shape[0]
    blk = ATTN_BLOCK
    n_blk = length // blk
    qi = lax.broadcasted_iota(jnp.int32, (blk, blk), 0)
    ki = lax.broadcasted_iota(jnp.int32, (blk, blk), 1)
    mask_first = ki <= qi
    qi2 = lax.broadcasted_iota(jnp.int32, (blk, 2 * blk), 0)
    ki2 = lax.broadcasted_iota(jnp.int32, (blk, 2 * blk), 1)
    dist = qi2 + blk - ki2
    mask_band = (dist >= 0) & (dist <= blk)
    lane = lax.broadcasted_iota(jnp.int32, (blk, LANES), 1)

    def one_block(q_rows, k_rows, mask):
        heads = [slice(hd * HEAD_DIM, (hd + 1) * HEAD_DIM) for hd in range(KV_HEADS)]
        scores = [_attn_scores(q_ref[q_rows, sl], k_ref[k_rows, sl]) for sl in heads]
        probs = [_attn_softmax(s, mask) for s in scores]
        lse_tile = jnp.zeros((blk, LANES), F32)
        for hd, sl in enumerate(heads):
            p, lse = probs[hd]
            o_ref[q_rows, sl] = _dot(p, v_ref[k_rows, sl]).astype(o_ref.dtype)
            lse_tile = jnp.where(lane == hd, lse, lse_tile)
        lse_ref[q_rows, :] = lse_tile

    one_block(pl.ds(0, blk), pl.ds(0, blk), mask_first)

    def body(n, carry):
        q0 = pl.multiple_of(n * blk, blk)
        k0 = pl.multiple_of((n - 1) * blk, blk)
        one_block(pl.ds(q0, blk), pl.ds(k0, 2 * blk), mask_band)
        return carry

    lax.fori_loop(1, n_blk, body, 0)


def dilated_attention(q, k, v, *, batch, seq, dilation):
    width = k.shape[-1]
    length = seq // dilation
    spec = pl.BlockSpec((None, None, length, width), lambda b, r: (b, r, 0, 0))
    return pl.pallas_call(
        _attn_kernel,
        grid=(batch, dilation),
        in_specs=[spec, spec, spec],
        out_specs=[spec, pl.BlockSpec((None, None, length, LANES), lambda b, r: (b, r, 0, 0))],
        out_shape=[
            jax.ShapeDtypeStruct((batch, dilation, length, width), BF16),
            jax.ShapeDtypeStruct((batch, dilation, length, LANES), F32),
        ],
        compiler_params=_cparams(("parallel", "parallel")),
        name=f"dilated_attention_d{dilation}",
    )(q, k, v)


def _split3_bf16(v):
    hi = v.astype(BF16)
    r1 = v - hi.astype(F32)
    mid = r1.astype(BF16)
    lo = (r1 - mid.astype(F32)).astype(BF16)
    return hi, mid, lo


def _attn_out_kernel(x_ref, o0_ref, o1_ref, o2_ref, l0_ref, l1_ref, l2_ref, unperm1_ref, unperm2_ref,
                     expand_ref, wo_ref, out_ref):
    tm, dm = x_ref.shape
    outs = [o0_ref[...].astype(F32)]
    lses = [l0_ref[...]]
    for o_ref, l_ref, unperm_ref in ((o1_ref, l1_ref, unperm1_ref), (o2_ref, l2_ref, unperm2_ref)):
        unperm = unperm_ref[...]
        outs.append(_dot(unperm, o_ref[...].reshape(tm, dm)))
        pieces = _split3_bf16(l_ref[...].reshape(tm, LANES))
        lses.append(_dot(unperm, pieces[0]) + _dot(unperm, pieces[1]) + _dot(unperm, pieces[2]))
    m = jnp.maximum(jnp.maximum(lses[0], lses[1]), lses[2])
    es = [jnp.exp(l - m) for l in lses]
    inv = 1.0 / (es[0] + es[1] + es[2])
    expand = expand_ref[...]
    mix = None
    for e, o in zip(es, outs):
        w = e * inv
        w_hi = w.astype(BF16)
        w_lo = (w - w_hi.astype(F32)).astype(BF16)
        w_full = _dot(w_hi, expand) + _dot(w_lo, expand)
        term = w_full * o
        mix = term if mix is None else mix + term
    out_ref[...] = x_ref[...] + _dot(mix.astype(BF16), wo_ref[...])


def attention_out(x, outs, lses, unperms, expand, w_o, *, batch, seq, tm=512):
    n_tok, dm = x.shape
    tps = seq // tm
    dils = [d for _, d in DILATION_GROUPS]
    row = lambda t: (t, 0)
    const = lambda t: (0, 0)
    o_specs = [pl.BlockSpec((tm, dm), row)] + [_residue_major_spec(tm, d, dm, tps) for d in dils[1:]]
    l_specs = [pl.BlockSpec((tm, LANES), row)] + [_residue_major_spec(tm, d, LANES, tps) for d in dils[1:]]
    o_args = [outs[0].reshape(n_tok, dm)] + [o.reshape(batch, d, tps, tm // d, dm) for o, d in zip(outs[1:], dils[1:])]
    l_args = [lses[0].reshape(n_tok, LANES)] + [l.reshape(batch, d, tps, tm // d, LANES)
                                                for l, d in zip(lses[1:], dils[1:])]
    return pl.pallas_call(
        _attn_out_kernel,
        grid=(n_tok // tm,),
        in_specs=[pl.BlockSpec((tm, dm), row)] + o_specs + l_specs
        + [pl.BlockSpec((tm, tm), const)] * 2 + [pl.BlockSpec((LANES, dm), const), pl.BlockSpec((dm, dm), const)],
        out_specs=pl.BlockSpec((tm, dm), row),
        out_shape=jax.ShapeDtypeStruct(x.shape, x.dtype),
        input_output_aliases={0: 0},
        compiler_params=_cparams(("parallel",)),
        name="attention_out",
    )(x, *o_args, *l_args, *unperms, expand, w_o)


def _rope_tables(positions):
    half = ROPE_DIMS // 2
    inv_freq = 1.0 / (ROPE_THETA ** (jnp.arange(0, ROPE_DIMS, 2, dtype=F32) / ROPE_DIMS))
    ang = positions.astype(F32).reshape(-1, 1) * inv_freq
    cos, sin = jnp.cos(ang), jnp.sin(ang)
    pad = HEAD_DIM - ROPE_DIMS
    cosf = jnp.concatenate([cos, cos, jnp.ones((cos.shape[0], pad), F32)], axis=-1)
    sinf = jnp.concatenate([-sin, sin, jnp.zeros((sin.shape[0], pad), F32)], axis=-1)
    assert half * 2 == ROPE_DIMS
    return cosf, sinf


def _residue_perm(tm, dilation):
    m = tm // dilation
    idx = jnp.arange(tm)
    src_row = (idx % m) * dilation + idx // m
    return (src_row[:, None] == jnp.arange(tm)[None, :]).astype(BF16)


def kernel(x, p, positions, a_norm, a_w_in, a_conv, a_w_out, kv_norm, w_kv, k_norm, b_norm, b_w_q, q_norm, b_w_o, ffn_norm, peer_w_q, peer_sub_keys, peer_u, peer_v, ple_norm, ple_w_gate, ple_w_proj):
    batch, seq, dm = x.shape
    depth = p.shape[0]
    n_a = a_norm.shape[0]
    n_tok = batch * seq
    tm = 512
    xf = x.reshape(n_tok, dm)
    cosf, sinf = _rope_tables(positions)
    head_of_lane = jnp.arange(dm) // HEAD_DIM
    expand = (jnp.arange(LANES)[:, None] == head_of_lane[None, :]).astype(BF16)
    dils = [d for _, d in DILATION_GROUPS]
    assert dils[0] == 1 and all(w // d == ATTN_BLOCK for w, d in DILATION_GROUPS)
    perms = [_residue_perm(tm, d) for d in dils[1:]]
    half = ROPE_DIMS // 2
    lane = jnp.arange(HEAD_DIM)
    src_lane = jnp.where(lane < half, lane + half, lane - half)
    rot_mat = ((jnp.arange(HEAD_DIM)[:, None] == src_lane[None, :]) & (lane < ROPE_DIMS)[None, :]).astype(BF16)
    unperms = [pm.T for pm in perms]

    def residue_major(a, d):
        return a.reshape(batch, d, seq // d, a.shape[-1])

    kvs = None
    for i in range(depth):
        if i < n_a:
            xf = conv_mixer(xf, a_norm[i][None], a_w_in[i].astype(BF16), a_conv[i], a_w_out[i].astype(BF16), seq=seq)
        else:
            j = i - n_a
            qs = query_proj(xf, b_norm[j][None], b_w_q[j].astype(BF16), q_norm[j][None], cosf, sinf, rot_mat, perms,
                            batch=batch, seq=seq, tm=tm)
            outs, lses = [], []
            for q_g, (k_g, v_g), d in zip(qs, kvs, dils):
                o_g, lse_g = dilated_attention(residue_major(q_g, d), residue_major(k_g, d), residue_major(v_g, d),
                                               batch=batch, seq=seq, dilation=d)
                outs.append(o_g)
                lses.append(lse_g)
            xf = attention_out(xf, outs, lses, unperms, expand, b_w_o[j].astype(BF16), batch=batch, seq=seq, tm=tm)
        ht, kap, alp, rho, bet = peer_route(xf, ffn_norm[i][None], peer_w_q[i].T.astype(BF16),
                                            peer_sub_keys[i].astype(BF16))
        xf = peer_dense(xf, ht, peer_u[i].astype(BF16), peer_v[i].T.astype(BF16), kap, alp, rho, bet,
                        p[i].reshape(n_tok, -1), ple_norm[i][None], ple_w_gate[i].astype(BF16),
                        ple_w_proj[i].astype(BF16))
        if i == n_a - 1:
            kvs = shared_kv(xf, kv_norm[None], w_kv.astype(BF16), k_norm[None], cosf, sinf, rot_mat, perms,
                            batch=batch, seq=seq, tm=tm)
    return xf.reshape(batch, seq, dm)
```

```python
import functools

import jax
import jax.numpy as jnp
from jax import lax
from jax.experimental import pallas as pl
from jax.experimental.pallas import tpu as pltpu

F32 = jnp.float32
BF16 = jnp.bfloat16
U32 = jnp.uint32

LANES = 128
SUBLANES = 8
VMEM_LIMIT_BYTES = 56 * 1024 * 1024

NORM_EPS = 1e-6
HEAD_DIM = 128
KV_HEADS = 8
N_GROUPS = 3
ROPE_DIMS = HEAD_DIM // 4
ROPE_THETA = 500000.0
DILATION_GROUPS = ((128, 1), (512, 4), (2048, 16))
PEER_HEADS = 8
PEER_N_KEYS = 128
PEER_TOPK = 16
INV_SQRT2 = 0.7071067811865476
NEG_INF = float("-inf")


def _cparams(sem):
    return pltpu.CompilerParams(dimension_semantics=sem, vmem_limit_bytes=VMEM_LIMIT_BYTES)


def _rms(x, gain):
    ms = jnp.mean(x * x, axis=-1, keepdims=True)
    return x * lax.rsqrt(ms + NORM_EPS) * gain


def _dot(a, b):
    return jnp.dot(a, b, preferred_element_type=F32)


def _conv_mixer_kernel(x_ref, gain_ref, win_ref, conv_ref, wout_ref, o_ref, gbuf_ref):
    j = pl.program_id(1)
    tm, dm = x_ref.shape
    x = x_ref[...]
    h = _rms(x, gain_ref[...]).astype(BF16)
    bcu = _dot(h, win_ref[...])
    b_gate = bcu[:, :dm]
    g = bcu[:, dm:2 * dm] * bcu[:, 2 * dm:]

    @pl.when(j == 0)
    def _():
        gbuf_ref[0:SUBLANES, :] = jnp.zeros((SUBLANES, dm), F32)

    gbuf_ref[SUBLANES:SUBLANES + tm, :] = g
    g1 = gbuf_ref[SUBLANES - 1:SUBLANES - 1 + tm, :]
    g2 = gbuf_ref[SUBLANES - 2:SUBLANES - 2 + tm, :]
    w = conv_ref[...]
    z = w[0:1, :] * g2 + w[1:2, :] * g1 + w[2:3, :] * g
    gbuf_ref[0:SUBLANES, :] = g[tm - SUBLANES:, :]
    y = _dot((b_gate * z).astype(BF16), wout_ref[...])
    o_ref[...] = x + y


def conv_mixer(x, gain, w_in, conv_w, w_out, *, seq, tm=512):
    n_tok, dm = x.shape
    tiles_per_seq = seq // tm
    row = lambda b, j: (b * tiles_per_seq + j, 0)
    const = lambda b, j: (0, 0)
    return pl.pallas_call(
        _conv_mixer_kernel,
        grid=(n_tok // seq, tiles_per_seq),
        in_specs=[
            pl.BlockSpec((tm, dm), row),
            pl.BlockSpec((1, dm), const),
            pl.BlockSpec((dm, 3 * dm), const),
            pl.BlockSpec(conv_w.shape, const),
            pl.BlockSpec((dm, dm), const),
        ],
        out_specs=pl.BlockSpec((tm, dm), row),
        out_shape=jax.ShapeDtypeStruct(x.shape, x.dtype),
        scratch_shapes=[pltpu.VMEM((tm + SUBLANES, dm), F32)],
        input_output_aliases={0: 0},
        compiler_params=_cparams(("arbitrary", "arbitrary")),
        name="conv_mixer",
    )(x, gain, w_in, conv_w, w_out)


def _staircase():
    return [(k1, k2) for k1 in range(PEER_TOPK) for k2 in range(PEER_TOPK) if (k1 + 1) * (k2 + 1) <= PEER_TOPK]


def _tree_max(vals):
    vals = list(vals)
    while len(vals) > 1:
        nxt = [jnp.maximum(vals[a], vals[a + 1]) for a in range(0, len(vals) - 1, 2)]
        if len(vals) % 2:
            nxt.append(vals[-1])
        vals = nxt
    return vals[0]


def _dup_bf16_bits(v):
    bits = lax.bitcast_convert_type(v, U32)
    return bits | (bits >> 16)


def _sorting_network(n):
    pairs = []
    p = 1
    while p < n:
        k = p
        while k >= 1:
            for j in range(k % p, n - k, 2 * k):
                for i in range(min(k, n - j - k)):
                    if (i + j) // (2 * p) == (i + j + k) // (2 * p):
                        pairs.append((i + j, i + j + k))
            k //= 2
        p *= 2
    return pairs


def _compare_exchange(xs, i, j):
    xs[i], xs[j] = jnp.maximum(xs[i], xs[j]), jnp.minimum(xs[i], xs[j])


def _top_sorted(s3):
    n = PEER_TOPK
    assert s3.shape[0] == n and s3.shape[1] == SUBLANES
    xs = [s3[v] for v in range(n)]
    for i, j in _sorting_network(n):
        _compare_exchange(xs, i, j)
    shift = SUBLANES // 2
    while shift >= 1:
        ys = [pltpu.roll(x, shift, 0) for x in xs]
        xs = [jnp.maximum(xs[k], ys[n - 1 - k]) for k in range(n)]
        d = n // 2
        while d >= 1:
            for k in range(n):
                if k & d == 0:
                    _compare_exchange(xs, k, k + d)
            d //= 2
        shift //= 2
    return xs


def _peer_route_kernel(x_ref, gain_ref, wqt_ref, keys_ref,
                       ht_ref, kap_ref, alp_ref, rho_ref, bet_ref, s1_ref, s2_ref):
    tr = x_ref.shape[0]
    groups = PEER_N_KEYS // SUBLANES
    h = _rms(x_ref[...], gain_ref[...])
    ht = h.T.astype(BF16)
    ht_ref[...] = ht
    qt = _dot(wqt_ref[...], ht).astype(BF16)

    tops = {}
    for hd in range(PEER_HEADS):
        for half in range(2):
            r0 = (hd * 2 + half) * PEER_N_KEYS
            s = _dot(keys_ref[hd, half], qt[r0:r0 + PEER_N_KEYS, :])
            (s1_ref if half == 0 else s2_ref)[hd] = s
            tops[hd, half] = _top_sorted(s.reshape(groups, SUBLANES, tr))

    a_k = [jnp.concatenate([tops[hd, 0][k][0:1, :] for hd in range(PEER_HEADS)], axis=0) for k in range(PEER_TOPK)]
    b_k = [jnp.concatenate([tops[hd, 1][k][0:1, :] for hd in range(PEER_HEADS)], axis=0) for k in range(PEER_TOPK)]
    stairs = _staircase()
    cands = [a_k[k1] + b_k[k2] for k1, k2 in stairs]
    work = cands
    for _ in range(PEER_TOPK - 1):
        m = _tree_max(work)
        work = [jnp.where(c == m, NEG_INF, c) for c in work]
    tau = _tree_max(work)
    top_sum = cands[0]
    z = None
    theta = [None] * PEER_TOPK
    for (k1, k2), c in zip(stairs, cands):
        sel = c >= tau
        e = jnp.where(sel, jnp.exp(c - top_sum), 0.0)
        z = e if z is None else z + e
        t = jnp.where(sel, a_k[k1], jnp.inf)
        theta[k2] = t if theta[k2] is None else jnp.minimum(theta[k2], t)
    inv_z = 0.5 / z

    for hd in range(PEER_HEADS):
        s1 = s1_ref[hd].reshape(groups, SUBLANES, tr)
        s2 = s2_ref[hd].reshape(groups, SUBLANES, tr)
        row = lambda v: jnp.broadcast_to(v[hd:hd + 1, :], (SUBLANES, tr))[None]
        kap = jnp.zeros(s1.shape, F32)
        for k2 in range(PEER_TOPK):
            kap = jnp.where(s1 >= row(theta[k2]), float(k2 + 1), kap)
        rho = jnp.full(s2.shape, float(PEER_TOPK), F32)
        for k in reversed(range(PEER_TOPK)):
            rho = jnp.where(s2 >= tops[hd, 1][k][None], float(k), rho)
        alp = jnp.exp(s1 - tops[hd, 0][0][None]).astype(BF16).astype(F32)
        bet = jnp.exp(s2 - tops[hd, 1][0][None]) * row(inv_z)
        kap_ref[hd] = _dup_bf16_bits(kap).reshape(PEER_N_KEYS, tr)
        alp_ref[hd] = _dup_bf16_bits(alp).reshape(PEER_N_KEYS, tr)
        rho_ref[hd] = rho.reshape(PEER_N_KEYS, tr).astype(BF16)
        bet_ref[hd] = bet.reshape(PEER_N_KEYS, tr).astype(BF16)


def peer_route(x, gain, wq_t, sub_keys, *, tr=256):
    n_tok, dm = x.shape
    nq = wq_t.shape[0]
    route_spec = pl.BlockSpec((PEER_HEADS, PEER_N_KEYS, tr), lambda t: (0, 0, t))
    route_shape = (PEER_HEADS, PEER_N_KEYS, n_tok)
    return pl.pallas_call(
        _peer_route_kernel,
        grid=(n_tok // tr,),
        in_specs=[
            pl.BlockSpec((tr, dm), lambda t: (t, 0)),
            pl.BlockSpec((1, dm), lambda t: (0, 0)),
            pl.BlockSpec((nq, dm), lambda t: (0, 0)),
            pl.BlockSpec(sub_keys.shape, lambda t: (0, 0, 0, 0)),
        ],
        out_specs=[pl.BlockSpec((dm, tr), lambda t: (0, t)), route_spec, route_spec, route_spec, route_spec],
        out_shape=[
            jax.ShapeDtypeStruct((dm, n_tok), BF16),
            jax.ShapeDtypeStruct(route_shape, U32),
            jax.ShapeDtypeStruct(route_shape, U32),
            jax.ShapeDtypeStruct(route_shape, BF16),
            jax.ShapeDtypeStruct(route_shape, BF16),
        ],
        scratch_shapes=[pltpu.VMEM((PEER_HEADS, PEER_N_KEYS, tr), F32)] * 2,
        compiler_params=_cparams(("parallel",)),
        name="peer_route",
    )(x, gain, wq_t, sub_keys)


def _ple(x, p, gain, w_gate, w_proj):
    h = _rms(x, gain).astype(BF16)
    return x + jax.nn.sigmoid(_dot(h, w_gate)) * _dot(p.astype(BF16), w_proj)


def _peer_dense_kernel(x_ref, ht_ref, u_ref, vt_ref, kap_ref, alp_ref, rho_in_ref, bet_in_ref,
                       p_ref, ple_gain_ref, wg_ref, wp_ref,
                       o_ref, acc_ref, rho_ref, bet_ref, *, sub_experts, lookahead):
    c = pl.program_id(1)
    te = vt_ref.shape[1]
    tm = ht_ref.shape[1]
    n_sub = te // sub_experts
    blocks_per_sub = sub_experts // PEER_N_KEYS
    blocks_per_chunk = te // PEER_N_KEYS
    assert blocks_per_chunk % SUBLANES == 0

    @pl.when(c == 0)
    def _():
        acc_ref[...] = jnp.zeros_like(acc_ref)
        rho_ref[...] = rho_in_ref[...]
        bet_ref[...] = bet_in_ref[...]

    ht = ht_ref[...]
    row0 = pl.multiple_of(c * blocks_per_chunk, SUBLANES)

    def expert_act(s):
        u = pltpu.bitcast(u_ref[pl.ds(s * sub_experts // 2, sub_experts // 2), :], BF16)
        return _dot(u, ht)

    def gated(s, act):
        act = act * (1.0 + lax.erf(act * INV_SQRT2))
        actb = act.astype(BF16)
        parts = []
        for b in range(blocks_per_sub):
            r = s * blocks_per_sub + b
            tile0 = row0 + (r // SUBLANES) * SUBLANES
            rr = r % SUBLANES
            gate = None
            for hd in range(PEER_HEADS):
                kap8 = kap_ref[hd, pl.ds(tile0, SUBLANES), :]
                alp8 = alp_ref[hd, pl.ds(tile0, SUBLANES), :]
                kap = pltpu.bitcast(jnp.broadcast_to(kap8[rr:rr + 1, :], (PEER_N_KEYS // 2, tm)), BF16)
                alp = pltpu.bitcast(jnp.broadcast_to(alp8[rr:rr + 1, :], (PEER_N_KEYS // 2, tm)), BF16)
                term = jnp.where(rho_ref[hd] < kap, alp * bet_ref[hd], jnp.zeros((), BF16))
                gate = term if gate is None else gate + term
            parts.append(gate * actb[b * PEER_N_KEYS:(b + 1) * PEER_N_KEYS, :])
        return jnp.concatenate(parts, axis=0) if len(parts) > 1 else parts[0]

    acts = {s: expert_act(s) for s in range(min(lookahead, n_sub))}
    for s in range(n_sub):
        if s + lookahead < n_sub:
            acts[s + lookahead] = expert_act(s + lookahead)
        p = gated(s, acts.pop(s))
        acc_ref[...] += _dot(pltpu.bitcast(vt_ref[:, pl.ds(s * sub_experts, sub_experts)], BF16), p)

    @pl.when(c == pl.num_programs(1) - 1)
    def _():
        x = x_ref[...] + acc_ref[...].T
        o_ref[...] = _ple(x, p_ref[...], ple_gain_ref[...], wg_ref[...], wp_ref[...])


def peer_dense(x, ht, u, vt, kap, alp, rho, bet, p_layers, layer, ple_gain, w_gate, w_proj, *,
               tm=512, te=2048, sub_experts=512, lookahead=3):
    n_tok, dm = x.shape
    n_exp = vt.shape[1]
    dp = p_layers.shape[-1]
    route_spec = pl.BlockSpec((PEER_HEADS, PEER_N_KEYS, tm), lambda t, c: (0, 0, t))
    const = lambda t, c: (0, 0)
    return pl.pallas_call(
        functools.partial(_peer_dense_kernel, sub_experts=sub_experts, lookahead=lookahead),
        grid=(n_tok // tm, n_exp // te),
        in_specs=[
            pl.BlockSpec((tm, dm), lambda t, c: (t, 0)),
            pl.BlockSpec((dm, tm), lambda t, c: (0, t)),
            pl.BlockSpec((te // 2, dm), lambda t, c: (c, 0)),
            pl.BlockSpec((dm // 2, te), lambda t, c: (0, c)),
            route_spec, route_spec, route_spec, route_spec,
            pl.BlockSpec((None, tm, dp), lambda t, c: (layer, t, 0)),
            pl.BlockSpec((1, dm), const),
            pl.BlockSpec((dm, dm), const),
            pl.BlockSpec((dp, dm), const),
        ],
        out_specs=pl.BlockSpec((tm, dm), lambda t, c: (t, 0)),
        out_shape=jax.ShapeDtypeStruct(x.shape, x.dtype),
        scratch_shapes=[
            pltpu.VMEM((dm, tm), F32),
            pltpu.VMEM((PEER_HEADS, PEER_N_KEYS, tm), BF16),
            pltpu.VMEM((PEER_HEADS, PEER_N_KEYS, tm), BF16),
        ],
        input_output_aliases={0: 0},
        compiler_params=_cparams(("parallel", "arbitrary")),
        name="peer_dense",
    )(x, ht, u, vt, kap, alp, rho, bet, p_layers, ple_gain, w_gate, w_proj)


def _head_norm_rope(y, gain, cosf, sinf, rot_mat):
    yn = _rms(y, gain)
    return yn * cosf + _dot(yn.astype(BF16), rot_mat) * sinf


def _store_residue_major(perm_ref, y, out_ref):
    d, m, w = out_ref.shape
    out_ref[...] = _dot(perm_ref[...], y).astype(out_ref.dtype).reshape(d, m, w)


def _kv_kernel(x_ref, gain_ref, w_ref, kgain_ref, cos_ref, sin_ref, rot_ref, perm1_ref, perm2_ref,
               k0_ref, v0_ref, k1_ref, v1_ref, k2_ref, v2_ref):
    dm = x_ref.shape[1]
    h = _rms(x_ref[...], gain_ref[...]).astype(BF16)
    kv = _dot(h, w_ref[...])
    v0_ref[...] = kv[:, dm:].astype(BF16)
    cosf, sinf, kg, rot = cos_ref[...], sin_ref[...], kgain_ref[...], rot_ref[...]
    for hd in range(KV_HEADS):
        sl = slice(hd * HEAD_DIM, (hd + 1) * HEAD_DIM)
        k0_ref[:, sl] = _head_norm_rope(kv[:, sl], kg, cosf, sinf, rot).astype(BF16)
    for perm_ref, k_ref, v_ref in ((perm1_ref, k1_ref, v1_ref), (perm2_ref, k2_ref, v2_ref)):
        _store_residue_major(perm_ref, k0_ref[...], k_ref)
        _store_residue_major(perm_ref, v0_ref[...], v_ref)


def _residue_major_spec(tm, dilation, width, tiles_per_seq):
    return pl.BlockSpec((None, dilation, None, tm // dilation, width),
                        lambda t: (t // tiles_per_seq, 0, t % tiles_per_seq, 0, 0))


def _residue_major_shape(batch, seq, tm, dilation, width, dtype):
    return jax.ShapeDtypeStruct((batch, dilation, seq // tm, tm // dilation, width), dtype)


def shared_kv(x, gain, w_kv, k_gain, cosf, sinf, rot_mat, perms, *, batch, seq, tm=512):
    n_tok, dm = x.shape
    tps = seq // tm
    row = pl.BlockSpec((tm, dm), lambda t: (t, 0))
    dils = [d for _, d in DILATION_GROUPS[1:]]
    const2 = lambda t: (0, 0)
    outs = pl.pallas_call(
        _kv_kernel,
        grid=(n_tok // tm,),
        in_specs=[
            row,
            pl.BlockSpec((1, dm), const2),
            pl.BlockSpec((dm, 2 * dm), const2),
            pl.BlockSpec((1, HEAD_DIM), const2),
            pl.BlockSpec((tm, HEAD_DIM), lambda t: (t, 0)),
            pl.BlockSpec((tm, HEAD_DIM), lambda t: (t, 0)),
            pl.BlockSpec((HEAD_DIM, HEAD_DIM), const2),
            pl.BlockSpec((tm, tm), const2),
            pl.BlockSpec((tm, tm), const2),
        ],
        out_specs=[row, row] + [_residue_major_spec(tm, d, dm, tps) for d in dils for _ in range(2)],
        out_shape=[jax.ShapeDtypeStruct((n_tok, dm), BF16)] * 2
        + [_residue_major_shape(batch, seq, tm, d, dm, BF16) for d in dils for _ in range(2)],
        compiler_params=_cparams(("parallel",)),
        name="shared_kv",
    )(x, gain, w_kv, k_gain, cosf, sinf, rot_mat, *perms)
    return [(outs[0], outs[1]), (outs[2], outs[3]), (outs[4], outs[5])]


def _q_kernel(x_ref, gain_ref, w_ref, qgain_ref, cos_ref, sin_ref, rot_ref, perm1_ref, perm2_ref,
              q0_ref, q1_ref, q2_ref, qbuf_ref):
    dm = x_ref.shape[1]
    h = _rms(x_ref[...], gain_ref[...]).astype(BF16)
    q = _dot(h, w_ref[...])
    cosf, sinf, qg, rot = cos_ref[...], sin_ref[...], qgain_ref[...], rot_ref[...]
    for hd in range(q.shape[1] // HEAD_DIM):
        sl = slice(hd * HEAD_DIM, (hd + 1) * HEAD_DIM)
        y = _head_norm_rope(q[:, sl], qg, cosf, sinf, rot).astype(BF16)
        if hd < KV_HEADS:
            q0_ref[:, sl] = y
        else:
            qbuf_ref[:, hd * HEAD_DIM - dm:(hd + 1) * HEAD_DIM - dm] = y
    _store_residue_major(perm1_ref, qbuf_ref[:, :dm], q1_ref)
    _store_residue_major(perm2_ref, qbuf_ref[:, dm:], q2_ref)


def query_proj(x, gain, w_q, q_gain, cosf, sinf, rot_mat, perms, *, batch, seq, tm=512):
    n_tok, dm = x.shape
    nq = w_q.shape[1]
    tps = seq // tm
    dils = [d for _, d in DILATION_GROUPS[1:]]
    const2 = lambda t: (0, 0)
    return pl.pallas_call(
        _q_kernel,
        grid=(n_tok // tm,),
        in_specs=[
            pl.BlockSpec((tm, dm), lambda t: (t, 0)),
            pl.BlockSpec((1, dm), const2),
            pl.BlockSpec((dm, nq), const2),
            pl.BlockSpec((1, HEAD_DIM), const2),
            pl.BlockSpec((tm, HEAD_DIM), lambda t: (t, 0)),
            pl.BlockSpec((tm, HEAD_DIM), lambda t: (t, 0)),
            pl.BlockSpec((HEAD_DIM, HEAD_DIM), const2),
            pl.BlockSpec((tm, tm), const2),
            pl.BlockSpec((tm, tm), const2),
        ],
        out_specs=[pl.BlockSpec((tm, dm), lambda t: (t, 0))] + [_residue_major_spec(tm, d, dm, tps) for d in dils],
        out_shape=[jax.ShapeDtypeStruct((n_tok, dm), BF16)]
        + [_residue_major_shape(batch, seq, tm, d, dm, BF16) for d in dils],
        scratch_shapes=[pltpu.VMEM((tm, nq - dm), BF16)],
        compiler_params=_cparams(("parallel",)),
        name="query_proj",
    )(x, gain, w_q, q_gain, cosf, sinf, rot_mat, *perms)


ATTN_BLOCK = 128


def _attn_scores(q, k):
    return lax.dot_general(q, k, (((1,), (1,)), ((), ())), preferred_element_type=F32) * (HEAD_DIM ** -0.5)


def _attn_softmax(s, mask):
    s = jnp.where(mask, s, NEG_INF)
    m = jnp.max(s, axis=-1, keepdims=True)
    e = jnp.exp(s - m)
    den = jnp.sum(e, axis=-1, keepdims=True)
    return (e / den).astype(BF16), m + jnp.log(den)


def _attn_kernel(q_ref, k_ref, v_ref, o_ref, lse_ref):
    length = q_ref.shape[0]
    blk = ATTN_BLOCK
    n_blk = length // blk
    qi = lax.broadcasted_iota(jnp.int32, (blk, blk), 0)
    ki = lax.broadcasted_iota(jnp.int32, (blk, blk), 1)
    mask_first = ki <= qi
    qi2 = lax.broadcasted_iota(jnp.int32, (blk, 2 * blk), 0)
    ki2 = lax.broadcasted_iota(jnp.int32, (blk, 2 * blk), 1)
    dist = qi2 + blk - ki2
    mask_band = (dist >= 0) & (dist <= blk)
    lane = lax.broadcasted_iota(jnp.int32, (blk, LANES), 1)

    def one_block(q_rows, k_rows, mask):
        heads = [slice(hd * HEAD_DIM, (hd + 1) * HEAD_DIM) for hd in range(KV_HEADS)]
        scores = [_attn_scores(q_ref[q_rows, sl], k_ref[k_rows, sl]) for sl in heads]
        probs = [_attn_softmax(s, mask) for s in scores]
        lse_tile = jnp.zeros((blk, LANES), F32)
        for hd, sl in enumerate(heads):
            p, lse = probs[hd]
            o_ref[q_rows, sl] = _dot(p, v_ref[k_rows, sl]).astype(o_ref.dtype)
            lse_tile = jnp.where(lane == hd, lse, lse_tile)
        lse_ref[q_rows, :] = lse_tile

    one_block(pl.ds(0, blk), pl.ds(0, blk), mask_first)

    def body(n, carry):
        q0 = pl.multiple_of(n * blk, blk)
        k0 = pl.multiple_of((n - 1) * blk, blk)
        one_block(pl.ds(q0, blk), pl.ds(k0, 2 * blk), mask_band)
        return carry

    lax.fori_loop(1, n_blk, body, 0)


def dilated_attention(q, k, v, *, batch, seq, dilation):
    width = k.shape[-1]
    length = seq // dilation
    spec = pl.BlockSpec((None, None, length, width), lambda b, r: (b, r, 0, 0))
    return pl.pallas_call(
        _attn_kernel,
        grid=(batch, dilation),
        in_specs=[spec, spec, spec],
        out_specs=[spec, pl.BlockSpec((None, None, length, LANES), lambda b, r: (b, r, 0, 0))],
        out_shape=[
            jax.ShapeDtypeStruct((batch, dilation, length, width), BF16),
            jax.ShapeDtypeStruct((batch, dilation, length, LANES), F32),
        ],
        compiler_params=_cparams(("parallel", "parallel")),
        name=f"dilated_attention_d{dilation}",
    )(q, k, v)


def _split3_bf16(v):
    hi = v.astype(BF16)
    r1 = v - hi.astype(F32)
    mid = r1.astype(BF16)
    lo = (r1 - mid.astype(F32)).astype(BF16)
    return hi, mid, lo


def _attn_out_kernel(x_ref, o0_ref, o1_ref, o2_ref, l0_ref, l1_ref, l2_ref, unperm1_ref, unperm2_ref,
                     expand_ref, wo_ref, out_ref):
    tm, dm = x_ref.shape
    outs = [o0_ref[...].astype(F32)]
    lses = [l0_ref[...]]
    for o_ref, l_ref, unperm_ref in ((o1_ref, l1_ref, unperm1_ref), (o2_ref, l2_ref, unperm2_ref)):
        unperm = unperm_ref[...]
        outs.append(_dot(unperm, o_ref[...].reshape(tm, dm)))
        pieces = _split3_bf16(l_ref[...].reshape(tm, LANES))
        lses.append(_dot(unperm, pieces[0]) + _dot(unperm, pieces[1]) + _dot(unperm, pieces[2]))
    m = jnp.maximum(jnp.maximum(lses[0], lses[1]), lses[2])
    es = [jnp.exp(l - m) for l in lses]
    inv = 1.0 / (es[0] + es[1] + es[2])
    expand = expand_ref[...]
    mix = None
    for e, o in zip(es, outs):
        w = e * inv
        w_hi = w.astype(BF16)
        w_lo = (w - w_hi.astype(F32)).astype(BF16)
        w_full = _dot(w_hi, expand) + _dot(w_lo, expand)
        term = w_full * o
        mix = term if mix is None else mix + term
    out_ref[...] = x_ref[...] + _dot(mix.astype(BF16), wo_ref[...])


def attention_out(x, outs, lses, unperms, expand, w_o, *, batch, seq, tm=512):
    n_tok, dm = x.shape
    tps = seq // tm
    dils = [d for _, d in DILATION_GROUPS]
    row = lambda t: (t, 0)
    const = lambda t: (0, 0)
    o_specs = [pl.BlockSpec((tm, dm), row)] + [_residue_major_spec(tm, d, dm, tps) for d in dils[1:]]
    l_specs = [pl.BlockSpec((tm, LANES), row)] + [_residue_major_spec(tm, d, LANES, tps) for d in dils[1:]]
    o_args = [outs[0].reshape(n_tok, dm)] + [o.reshape(batch, d, tps, tm // d, dm) for o, d in zip(outs[1:], dils[1:])]
    l_args = [lses[0].reshape(n_tok, LANES)] + [l.reshape(batch, d, tps, tm // d, LANES)
                                                for l, d in zip(lses[1:], dils[1:])]
    return pl.pallas_call(
        _attn_out_kernel,
        grid=(n_tok // tm,),
        in_specs=[pl.BlockSpec((tm, dm), row)] + o_specs + l_specs
        + [pl.BlockSpec((tm, tm), const)] * 2 + [pl.BlockSpec((LANES, dm), const), pl.BlockSpec((dm, dm), const)],
        out_specs=pl.BlockSpec((tm, dm), row),
        out_shape=jax.ShapeDtypeStruct(x.shape, x.dtype),
        input_output_aliases={0: 0},
        compiler_params=_cparams(("parallel",)),
        name="attention_out",
    )(x, *o_args, *l_args, *unperms, expand, w_o)


def _pack_kernel(w_ref, o_ref, *, transpose):
    w = w_ref[...]
    if transpose:
        w = w.T
    o_ref[...] = pltpu.bitcast(w.astype(BF16), U32)


def pack_row_pairs(w_layers, layer, *, transpose=False, block=1024):
    _, rows, cols = w_layers.shape
    if transpose:
        out_shape = (cols // 2, rows)
        out_spec = pl.BlockSpec((cols // 2, block), lambda j: (0, j))
    else:
        out_shape = (rows // 2, cols)
        out_spec = pl.BlockSpec((block // 2, cols), lambda j: (j, 0))
    return pl.pallas_call(
        functools.partial(_pack_kernel, transpose=transpose),
        grid=(rows // block,),
        in_specs=[pl.BlockSpec((None, block, cols), lambda j: (layer, j, 0))],
        out_specs=out_spec,
        out_shape=jax.ShapeDtypeStruct(out_shape, U32),
        compiler_params=_cparams(("parallel",)),
        name="pack_weights_t" if transpose else "pack_weights",
    )(w_layers)


def _rope_tables(positions):
    half = ROPE_DIMS // 2
    inv_freq = 1.0 / (ROPE_THETA ** (jnp.arange(0, ROPE_DIMS, 2, dtype=F32) / ROPE_DIMS))
    ang = positions.astype(F32).reshape(-1, 1) * inv_freq
    cos, sin = jnp.cos(ang), jnp.sin(ang)
    pad = HEAD_DIM - ROPE_DIMS
    cosf = jnp.concatenate([cos, cos, jnp.ones((cos.shape[0], pad), F32)], axis=-1)
    sinf = jnp.concatenate([-sin, sin, jnp.zeros((sin.shape[0], pad), F32)], axis=-1)
    assert half * 2 == ROPE_DIMS
    return cosf, sinf


def _residue_perm(tm, dilation):
    m = tm // dilation
    idx = jnp.arange(tm)
    src_row = (idx % m) * dilation + idx // m
    return (src_row[:, None] == jnp.arange(tm)[None, :]).astype(BF16)


def kernel(x, p, positions, a_norm, a_w_in, a_conv, a_w_out, kv_norm, w_kv, k_norm, b_norm, b_w_q, q_norm, b_w_o, ffn_norm, peer_w_q, peer_sub_keys, peer_u, peer_v, ple_norm, ple_w_gate, ple_w_proj):
    batch, seq, dm = x.shape
    depth = p.shape[0]
    n_a = a_norm.shape[0]
    n_tok = batch * seq
    tm = 512
    xf = x.reshape(n_tok, dm)
    cosf, sinf = _rope_tables(positions)
    head_of_lane = jnp.arange(dm) // HEAD_DIM
    expand = (jnp.arange(LANES)[:, None] == head_of_lane[None, :]).astype(BF16)
    dils = [d for _, d in DILATION_GROUPS]
    assert dils[0] == 1 and all(w // d == ATTN_BLOCK for w, d in DILATION_GROUPS)
    perms = [_residue_perm(tm, d) for d in dils[1:]]
    half = ROPE_DIMS // 2
    lane = jnp.arange(HEAD_DIM)
    src_lane = jnp.where(lane < half, lane + half, lane - half)
    rot_mat = ((jnp.arange(HEAD_DIM)[:, None] == src_lane[None, :]) & (lane < ROPE_DIMS)[None, :]).astype(BF16)
    unperms = [pm.T for pm in perms]

    def residue_major(a, d):
        return a.reshape(batch, d, seq // d, a.shape[-1])

    p_flat = p.reshape(depth, n_tok, -1)
    kvs = None
    for i in range(depth):
        if i < n_a:
            xf = conv_mixer(xf, a_norm[i][None], a_w_in[i].astype(BF16), a_conv[i], a_w_out[i].astype(BF16), seq=seq)
        else:
            j = i - n_a
            qs = query_proj(xf, b_norm[j][None], b_w_q[j].astype(BF16), q_norm[j][None], cosf, sinf, rot_mat, perms,
                            batch=batch, seq=seq, tm=tm)
            outs, lses = [], []
            for q_g, (k_g, v_g), d in zip(qs, kvs, dils):
                o_g, lse_g = dilated_attention(residue_major(q_g, d), residue_major(k_g, d), residue_major(v_g, d),
                                               batch=batch, seq=seq, dilation=d)
                outs.append(o_g)
                lses.append(lse_g)
            xf = attention_out(xf, outs, lses, unperms, expand, b_w_o[j].astype(BF16), batch=batch, seq=seq, tm=tm)
        ht, kap, alp, rho, bet = peer_route(xf, ffn_norm[i][None], peer_w_q[i].T.astype(BF16),
                                            peer_sub_keys[i].astype(BF16))
        xf = peer_dense(xf, ht, pack_row_pairs(peer_u, i), pack_row_pairs(peer_v, i, transpose=True),
                        kap, alp, rho, bet, p_flat, i, ple_norm[i][None], ple_w_gate[i].astype(BF16),
                        ple_w_proj[i].astype(BF16))
        if i == n_a - 1:
            kvs = shared_kv(xf, kv_norm[None], w_kv.astype(BF16), k_norm[None], cosf, sinf, rot_mat, perms,
                            batch=batch, seq=seq, tm=tm)
    return xf.reshape(batch, seq, dm)
```

```python
import functools

import jax
import jax.numpy as jnp
from jax import lax
from jax.experimental import pallas as pl
from jax.experimental.pallas import tpu as pltpu

F32 = jnp.float32
BF16 = jnp.bfloat16
U32 = jnp.uint32

LANES = 128
SUBLANES = 8
VMEM_LIMIT_BYTES = 56 * 1024 * 1024

NORM_EPS = 1e-6
HEAD_DIM = 128
KV_HEADS = 8
N_GROUPS = 3
ROPE_DIMS = HEAD_DIM // 4
ROPE_THETA = 500000.0
DILATION_GROUPS = ((128, 1), (512, 4), (2048, 16))
PEER_HEADS = 8
PEER_N_KEYS = 128
PEER_TOPK = 16
INV_SQRT2 = 0.7071067811865476
NEG_INF = float("-inf")


def _cparams(sem):
    return pltpu.CompilerParams(dimension_semantics=sem, vmem_limit_bytes=VMEM_LIMIT_BYTES)


def _rms(x, gain):
    ms = jnp.mean(x * x, axis=-1, keepdims=True)
    return x * lax.rsqrt(ms + NORM_EPS) * gain


def _dot(a, b):
    return jnp.dot(a, b, preferred_element_type=F32)


def _conv_mixer_kernel(x_ref, gain_ref, win_ref, conv_ref, wout_ref, o_ref, gbuf_ref):
    j = pl.program_id(1)
    tm, dm = x_ref.shape
    x = x_ref[...]
    h = _rms(x, gain_ref[...]).astype(BF16)
    bcu = _dot(h, win_ref[...])
    b_gate = bcu[:, :dm]
    g = bcu[:, dm:2 * dm] * bcu[:, 2 * dm:]

    @pl.when(j == 0)
    def _():
        gbuf_ref[0:SUBLANES, :] = jnp.zeros((SUBLANES, dm), F32)

    gbuf_ref[SUBLANES:SUBLANES + tm, :] = g
    g1 = gbuf_ref[SUBLANES - 1:SUBLANES - 1 + tm, :]
    g2 = gbuf_ref[SUBLANES - 2:SUBLANES - 2 + tm, :]
    w = conv_ref[...]
    z = w[0:1, :] * g2 + w[1:2, :] * g1 + w[2:3, :] * g
    gbuf_ref[0:SUBLANES, :] = g[tm - SUBLANES:, :]
    y = _dot((b_gate * z).astype(BF16), wout_ref[...])
    o_ref[...] = x + y


def conv_mixer(x, gain, w_in, conv_w, w_out, *, seq, in_place, tm=512):
    n_tok, dm = x.shape
    tiles_per_seq = seq // tm
    row = lambda b, j: (b * tiles_per_seq + j, 0)
    const = lambda b, j: (0, 0)
    return pl.pallas_call(
        _conv_mixer_kernel,
        grid=(n_tok // seq, tiles_per_seq),
        in_specs=[
            pl.BlockSpec((tm, dm), row),
            pl.BlockSpec((1, dm), const),
            pl.BlockSpec((dm, 3 * dm), const),
            pl.BlockSpec(conv_w.shape, const),
            pl.BlockSpec((dm, dm), const),
        ],
        out_specs=pl.BlockSpec((tm, dm), row),
        out_shape=jax.ShapeDtypeStruct(x.shape, x.dtype),
        scratch_shapes=[pltpu.VMEM((tm + SUBLANES, dm), F32)],
        input_output_aliases={0: 0} if in_place else {},
        compiler_params=_cparams(("arbitrary", "arbitrary")),
        name="conv_mixer",
    )(x, gain, w_in, conv_w, w_out)


def _staircase():
    return [(k1, k2) for k1 in range(PEER_TOPK) for k2 in range(PEER_TOPK) if (k1 + 1) * (k2 + 1) <= PEER_TOPK]


def _tree_max(vals):
    vals = list(vals)
    while len(vals) > 1:
        nxt = [jnp.maximum(vals[a], vals[a + 1]) for a in range(0, len(vals) - 1, 2)]
        if len(vals) % 2:
            nxt.append(vals[-1])
        vals = nxt
    return vals[0]


def _dup_bf16_bits(v):
    bits = lax.bitcast_convert_type(v, U32)
    return bits | (bits >> 16)


def _sorting_network(n):
    pairs = []
    p = 1
    while p < n:
        k = p
        while k >= 1:
            for j in range(k % p, n - k, 2 * k):
                for i in range(min(k, n - j - k)):
                    if (i + j) // (2 * p) == (i + j + k) // (2 * p):
                        pairs.append((i + j, i + j + k))
            k //= 2
        p *= 2
    return pairs


def _compare_exchange(xs, i, j):
    xs[i], xs[j] = jnp.maximum(xs[i], xs[j]), jnp.minimum(xs[i], xs[j])


def _top_sorted(s3):
    n = PEER_TOPK
    assert s3.shape[0] == n and s3.shape[1] == SUBLANES
    xs = [s3[v] for v in range(n)]
    for i, j in _sorting_network(n):
        _compare_exchange(xs, i, j)
    shift = SUBLANES // 2
    while shift >= 1:
        ys = [pltpu.roll(x, shift, 0) for x in xs]
        xs = [jnp.maximum(xs[k], ys[n - 1 - k]) for k in range(n)]
        d = n // 2
        while d >= 1:
            for k in range(n):
                if k & d == 0:
                    _compare_exchange(xs, k, k + d)
            d //= 2
        shift //= 2
    return xs


def _peer_route_kernel(x_ref, gain_ref, wqt_ref, keys_ref,
                       ht_ref, kap_ref, alp_ref, rho_ref, bet_ref, s1_ref, s2_ref):
    tr = x_ref.shape[0]
    groups = PEER_N_KEYS // SUBLANES
    h = _rms(x_ref[...], gain_ref[...])
    ht = h.T.astype(BF16)
    ht_ref[...] = ht
    qt = _dot(wqt_ref[...], ht).astype(BF16)

    tops = {}
    for hd in range(PEER_HEADS):
        for half in range(2):
            r0 = (hd * 2 + half) * PEER_N_KEYS
            s = _dot(keys_ref[hd, half], qt[r0:r0 + PEER_N_KEYS, :])
            (s1_ref if half == 0 else s2_ref)[hd] = s
            tops[hd, half] = _top_sorted(s.reshape(groups, SUBLANES, tr))

    a_k = [jnp.concatenate([tops[hd, 0][k][0:1, :] for hd in range(PEER_HEADS)], axis=0) for k in range(PEER_TOPK)]
    b_k = [jnp.concatenate([tops[hd, 1][k][0:1, :] for hd in range(PEER_HEADS)], axis=0) for k in range(PEER_TOPK)]
    stairs = _staircase()
    cands = [a_k[k1] + b_k[k2] for k1, k2 in stairs]
    work = cands
    for _ in range(PEER_TOPK - 1):
        m = _tree_max(work)
        work = [jnp.where(c == m, NEG_INF, c) for c in work]
    tau = _tree_max(work)
    top_sum = cands[0]
    z = None
    theta = [None] * PEER_TOPK
    for (k1, k2), c in zip(stairs, cands):
        sel = c >= tau
        e = jnp.where(sel, jnp.exp(c - top_sum), 0.0)
        z = e if z is None else z + e
        t = jnp.where(sel, a_k[k1], jnp.inf)
        theta[k2] = t if theta[k2] is None else jnp.minimum(theta[k2], t)
    inv_z = 0.5 / z

    for hd in range(PEER_HEADS):
        s1 = s1_ref[hd].reshape(groups, SUBLANES, tr)
        s2 = s2_ref[hd].reshape(groups, SUBLANES, tr)
        row = lambda v: jnp.broadcast_to(v[hd:hd + 1, :], (SUBLANES, tr))[None]
        kap = jnp.zeros(s1.shape, F32)
        for k2 in range(PEER_TOPK):
            kap = jnp.where(s1 >= row(theta[k2]), float(k2 + 1), kap)
        rho = jnp.full(s2.shape, float(PEER_TOPK), F32)
        for k in reversed(range(PEER_TOPK)):
            rho = jnp.where(s2 >= tops[hd, 1][k][None], float(k), rho)
        alp = jnp.exp(s1 - tops[hd, 0][0][None]).astype(BF16).astype(F32)
        bet = jnp.exp(s2 - tops[hd, 1][0][None]) * row(inv_z)
        kap_ref[hd] = _dup_bf16_bits(kap).reshape(PEER_N_KEYS, tr)
        alp_ref[hd] = _dup_bf16_bits(alp).reshape(PEER_N_KEYS, tr)
        rho_ref[hd] = rho.reshape(PEER_N_KEYS, tr).astype(BF16)
        bet_ref[hd] = bet.reshape(PEER_N_KEYS, tr).astype(BF16)


def peer_route(x, gain, wq_t, sub_keys, *, tr=256):
    n_tok, dm = x.shape
    nq = wq_t.shape[0]
    route_spec = pl.BlockSpec((PEER_HEADS, PEER_N_KEYS, tr), lambda t: (0, 0, t))
    route_shape = (PEER_HEADS, PEER_N_KEYS, n_tok)
    return pl.pallas_call(
        _peer_route_kernel,
        grid=(n_tok // tr,),
        in_specs=[
            pl.BlockSpec((tr, dm), lambda t: (t, 0)),
            pl.BlockSpec((1, dm), lambda t: (0, 0)),
            pl.BlockSpec((nq, dm), lambda t: (0, 0)),
            pl.BlockSpec(sub_keys.shape, lambda t: (0, 0, 0, 0)),
        ],
        out_specs=[pl.BlockSpec((dm, tr), lambda t: (0, t)), route_spec, route_spec, route_spec, route_spec],
        out_shape=[
            jax.ShapeDtypeStruct((dm, n_tok), BF16),
            jax.ShapeDtypeStruct(route_shape, U32),
            jax.ShapeDtypeStruct(route_shape, U32),
            jax.ShapeDtypeStruct(route_shape, BF16),
            jax.ShapeDtypeStruct(route_shape, BF16),
        ],
        scratch_shapes=[pltpu.VMEM((PEER_HEADS, PEER_N_KEYS, tr), F32)] * 2,
        compiler_params=_cparams(("parallel",)),
        name="peer_route",
    )(x, gain, wq_t, sub_keys)


def _ple(x, p, gain, w_gate, w_proj):
    h = _rms(x, gain).astype(BF16)
    return x + jax.nn.sigmoid(_dot(h, w_gate)) * _dot(p.astype(BF16), w_proj)


def _peer_dense_kernel(x_ref, ht_ref, u_ref, vt_ref, kap_ref, alp_ref, rho_in_ref, bet_in_ref,
                       p_ref, ple_gain_ref, wg_ref, wp_ref,
                       o_ref, acc_ref, rho_ref, bet_ref, *, sub_experts, lookahead):
    c = pl.program_id(1)
    te = vt_ref.shape[1]
    tm = ht_ref.shape[1]
    n_sub = te // sub_experts
    blocks_per_sub = sub_experts // PEER_N_KEYS
    blocks_per_chunk = te // PEER_N_KEYS
    assert blocks_per_chunk % SUBLANES == 0

    @pl.when(c == 0)
    def _():
        acc_ref[...] = jnp.zeros_like(acc_ref)
        rho_ref[...] = rho_in_ref[...]
        bet_ref[...] = bet_in_ref[...]

    ht = ht_ref[...]
    row0 = pl.multiple_of(c * blocks_per_chunk, SUBLANES)

    def expert_act(s):
        u = pltpu.bitcast(u_ref[pl.ds(s * sub_experts // 2, sub_experts // 2), :], BF16)
        return _dot(u, ht)

    def gated(s, act):
        act = act * (1.0 + lax.erf(act * INV_SQRT2))
        actb = act.astype(BF16)
        parts = []
        for b in range(blocks_per_sub):
            r = s * blocks_per_sub + b
            tile0 = row0 + (r // SUBLANES) * SUBLANES
            rr = r % SUBLANES
            gate = None
            for hd in range(PEER_HEADS):
                kap8 = kap_ref[hd, pl.ds(tile0, SUBLANES), :]
                alp8 = alp_ref[hd, pl.ds(tile0, SUBLANES), :]
                kap = pltpu.bitcast(jnp.broadcast_to(kap8[rr:rr + 1, :], (PEER_N_KEYS // 2, tm)), BF16)
                alp = pltpu.bitcast(jnp.broadcast_to(alp8[rr:rr + 1, :], (PEER_N_KEYS // 2, tm)), BF16)
                term = jnp.where(rho_ref[hd] < kap, alp * bet_ref[hd], jnp.zeros((), BF16))
                gate = term if gate is None else gate + term
            parts.append(gate * actb[b * PEER_N_KEYS:(b + 1) * PEER_N_KEYS, :])
        return jnp.concatenate(parts, axis=0) if len(parts) > 1 else parts[0]

    acts = {s: expert_act(s) for s in range(min(lookahead, n_sub))}
    for s in range(n_sub):
        if s + lookahead < n_sub:
            acts[s + lookahead] = expert_act(s + lookahead)
        p = gated(s, acts.pop(s))
        acc_ref[...] += _dot(pltpu.bitcast(vt_ref[:, pl.ds(s * sub_experts, sub_experts)], BF16), p)

    @pl.when(c == pl.num_programs(1) - 1)
    def _():
        x = x_ref[...] + acc_ref[...].T
        o_ref[...] = _ple(x, p_ref[...], ple_gain_ref[...], wg_ref[...], wp_ref[...])


def peer_dense(x, ht, u, vt, kap, alp, rho, bet, p_layers, layer, ple_gain, w_gate, w_proj, *,
               tm=512, te=2048, sub_experts=512, lookahead=3):
    n_tok, dm = x.shape
    n_exp = vt.shape[1]
    dp = p_layers.shape[-1]
    route_spec = pl.BlockSpec((PEER_HEADS, PEER_N_KEYS, tm), lambda t, c: (0, 0, t))
    const = lambda t, c: (0, 0)
    return pl.pallas_call(
        functools.partial(_peer_dense_kernel, sub_experts=sub_experts, lookahead=lookahead),
        grid=(n_tok // tm, n_exp // te),
        in_specs=[
            pl.BlockSpec((tm, dm), lambda t, c: (t, 0)),
            pl.BlockSpec((dm, tm), lambda t, c: (0, t)),
            pl.BlockSpec((te // 2, dm), lambda t, c: (c, 0)),
            pl.BlockSpec((dm // 2, te), lambda t, c: (0, c)),
            route_spec, route_spec, route_spec, route_spec,
            pl.BlockSpec((None, tm, dp), lambda t, c: (layer, t, 0)),
            pl.BlockSpec((1, dm), const),
            pl.BlockSpec((dm, dm), const),
            pl.BlockSpec((dp, dm), const),
        ],
        out_specs=pl.BlockSpec((tm, dm), lambda t, c: (t, 0)),
        out_shape=jax.ShapeDtypeStruct(x.shape, x.dtype),
        scratch_shapes=[
            pltpu.VMEM((dm, tm), F32),
            pltpu.VMEM((PEER_HEADS, PEER_N_KEYS, tm), BF16),
            pltpu.VMEM((PEER_HEADS, PEER_N_KEYS, tm), BF16),
        ],
        input_output_aliases={0: 0},
        compiler_params=_cparams(("parallel", "arbitrary")),
        name="peer_dense",
    )(x, ht, u, vt, kap, alp, rho, bet, p_layers, ple_gain, w_gate, w_proj)


def _head_norm_rope(y, gain, cosf, sinf, rot_mat):
    yn = _rms(y, gain)
    return yn * cosf + _dot(yn.astype(BF16), rot_mat) * sinf


def _store_residue_major(perm_ref, y, out_ref):
    d, m, w = out_ref.shape
    out_ref[...] = _dot(perm_ref[...], y).astype(out_ref.dtype).reshape(d, m, w)


def _kv_kernel(x_ref, gain_ref, w_ref, kgain_ref, cos_ref, sin_ref, rot_ref, perm1_ref, perm2_ref,
               k0_ref, v0_ref, k1_ref, v1_ref, k2_ref, v2_ref):
    dm = x_ref.shape[1]
    h = _rms(x_ref[...], gain_ref[...]).astype(BF16)
    kv = _dot(h, w_ref[...])
    v0_ref[...] = kv[:, dm:].astype(BF16)
    cosf, sinf, kg, rot = cos_ref[...], sin_ref[...], kgain_ref[...], rot_ref[...]
    for hd in range(KV_HEADS):
        sl = slice(hd * HEAD_DIM, (hd + 1) * HEAD_DIM)
        k0_ref[:, sl] = _head_norm_rope(kv[:, sl], kg, cosf, sinf, rot).astype(BF16)
    for perm_ref, k_ref, v_ref in ((perm1_ref, k1_ref, v1_ref), (perm2_ref, k2_ref, v2_ref)):
        _store_residue_major(perm_ref, k0_ref[...], k_ref)
        _store_residue_major(perm_ref, v0_ref[...], v_ref)


def _residue_major_spec(tm, dilation, width, tiles_per_seq):
    return pl.BlockSpec((None, dilation, None, tm // dilation, width),
                        lambda t: (t // tiles_per_seq, 0, t % tiles_per_seq, 0, 0))


def _residue_major_shape(batch, seq, tm, dilation, width, dtype):
    return jax.ShapeDtypeStruct((batch, dilation, seq // tm, tm // dilation, width), dtype)


def shared_kv(x, gain, w_kv, k_gain, cosf, sinf, rot_mat, perms, *, batch, seq, tm=512):
    n_tok, dm = x.shape
    tps = seq // tm
    row = pl.BlockSpec((tm, dm), lambda t: (t, 0))
    dils = [d for _, d in DILATION_GROUPS[1:]]
    const2 = lambda t: (0, 0)
    outs = pl.pallas_call(
        _kv_kernel,
        grid=(n_tok // tm,),
        in_specs=[
            row,
            pl.BlockSpec((1, dm), const2),
            pl.BlockSpec((dm, 2 * dm), const2),
            pl.BlockSpec((1, HEAD_DIM), const2),
            pl.BlockSpec((tm, HEAD_DIM), lambda t: (t, 0)),
            pl.BlockSpec((tm, HEAD_DIM), lambda t: (t, 0)),
            pl.BlockSpec((HEAD_DIM, HEAD_DIM), const2),
            pl.BlockSpec((tm, tm), const2),
            pl.BlockSpec((tm, tm), const2),
        ],
        out_specs=[row, row] + [_residue_major_spec(tm, d, dm, tps) for d in dils for _ in range(2)],
        out_shape=[jax.ShapeDtypeStruct((n_tok, dm), BF16)] * 2
        + [_residue_major_shape(batch, seq, tm, d, dm, BF16) for d in dils for _ in range(2)],
        compiler_params=_cparams(("parallel",)),
        name="shared_kv",
    )(x, gain, w_kv, k_gain, cosf, sinf, rot_mat, *perms)
    return [(outs[0], outs[1]), (outs[2], outs[3]), (outs[4], outs[5])]


def _q_kernel(x_ref, gain_ref, w_ref, qgain_ref, cos_ref, sin_ref, rot_ref, perm1_ref, perm2_ref,
              q0_ref, q1_ref, q2_ref, qbuf_ref):
    dm = x_ref.shape[1]
    h = _rms(x_ref[...], gain_ref[...]).astype(BF16)
    q = _dot(h, w_ref[...])
    cosf, sinf, qg, rot = cos_ref[...], sin_ref[...], qgain_ref[...], rot_ref[...]
    for hd in range(q.shape[1] // HEAD_DIM):
        sl = slice(hd * HEAD_DIM, (hd + 1) * HEAD_DIM)
        y = _head_norm_rope(q[:, sl], qg, cosf, sinf, rot).astype(BF16)
        if hd < KV_HEADS:
            q0_ref[:, sl] = y
        else:
            qbuf_ref[:, hd * HEAD_DIM - dm:(hd + 1) * HEAD_DIM - dm] = y
    _store_residue_major(perm1_ref, qbuf_ref[:, :dm], q1_ref)
    _store_residue_major(perm2_ref, qbuf_ref[:, dm:], q2_ref)


def query_proj(x, gain, w_q, q_gain, cosf, sinf, rot_mat, perms, *, batch, seq, tm=512):
    n_tok, dm = x.shape
    nq = w_q.shape[1]
    tps = seq // tm
    dils = [d for _, d in DILATION_GROUPS[1:]]
    const2 = lambda t: (0, 0)
    return pl.pallas_call(
        _q_kernel,
        grid=(n_tok // tm,),
        in_specs=[
            pl.BlockSpec((tm, dm), lambda t: (t, 0)),
            pl.BlockSpec((1, dm), const2),
            pl.BlockSpec((dm, nq), const2),
            pl.BlockSpec((1, HEAD_DIM), const2),
            pl.BlockSpec((tm, HEAD_DIM), lambda t: (t, 0)),
            pl.BlockSpec((tm, HEAD_DIM), lambda t: (t, 0)),
            pl.BlockSpec((HEAD_DIM, HEAD_DIM), const2),
            pl.BlockSpec((tm, tm), const2),
            pl.BlockSpec((tm, tm), const2),
        ],
        out_specs=[pl.BlockSpec((tm, dm), lambda t: (t, 0))] + [_residue_major_spec(tm, d, dm, tps) for d in dils],
        out_shape=[jax.ShapeDtypeStruct((n_tok, dm), BF16)]
        + [_residue_major_shape(batch, seq, tm, d, dm, BF16) for d in dils],
        scratch_shapes=[pltpu.VMEM((tm, nq - dm), BF16)],
        compiler_params=_cparams(("parallel",)),
        name="query_proj",
    )(x, gain, w_q, q_gain, cosf, sinf, rot_mat, *perms)


ATTN_BLOCK = 128
ATTN_ROWS_PER_STEP = 1024
ATTN_MAX_RESIDUES_PER_STEP = 4


def _attn_scores(q, k):
    return lax.dot_general(q, k, (((1,), (1,)), ((), ())), preferred_element_type=F32) * (HEAD_DIM ** -0.5)


def _attn_softmax(s, mask):
    s = jnp.where(mask, s, NEG_INF)
    m = jnp.max(s, axis=-1, keepdims=True)
    e = jnp.exp(s - m)
    den = jnp.sum(e, axis=-1, keepdims=True)
    return (e / den).astype(BF16), m + jnp.log(den)


def _attn_kernel(q_ref, k_ref, v_ref, o_ref, lse_ref):
    n_res, length = q_ref.shape[0], q_ref.shape[1]
    blk = ATTN_BLOCK
    n_blk = length // blk
    qi = lax.broadcasted_iota(jnp.int32, (blk, blk), 0)
    ki = lax.broadcasted_iota(jnp.int32, (blk, blk), 1)
    mask_first = ki <= qi
    qi2 = lax.broadcasted_iota(jnp.int32, (blk, 2 * blk), 0)
    ki2 = lax.broadcasted_iota(jnp.int32, (blk, 2 * blk), 1)
    dist = qi2 + blk - ki2
    mask_band = (dist >= 0) & (dist <= blk)
    lane = lax.broadcasted_iota(jnp.int32, (blk, LANES), 1)

    def one_block(q_rows, k_rows, mask):
        heads = [slice(hd * HEAD_DIM, (hd + 1) * HEAD_DIM) for hd in range(KV_HEADS)]
        scores = [[_attn_scores(q_ref[r, q_rows, sl], k_ref[r, k_rows, sl]) for sl in heads] for r in range(n_res)]
        probs = [[_attn_softmax(s, mask) for s in row] for row in scores]
        for r in range(n_res):
            lse_tile = jnp.zeros((blk, LANES), F32)
            for hd, sl in enumerate(heads):
                p, lse = probs[r][hd]
                o_ref[r, q_rows, sl] = _dot(p, v_ref[r, k_rows, sl]).astype(o_ref.dtype)
                lse_tile = jnp.where(lane == hd, lse, lse_tile)
            lse_ref[r, q_rows, :] = lse_tile

    one_block(pl.ds(0, blk), pl.ds(0, blk), mask_first)

    def body(n, carry):
        q0 = pl.multiple_of(n * blk, blk)
        k0 = pl.multiple_of((n - 1) * blk, blk)
        one_block(pl.ds(q0, blk), pl.ds(k0, 2 * blk), mask_band)
        return carry

    lax.fori_loop(1, n_blk, body, 0)


def dilated_attention(q, k, v, *, batch, seq, dilation):
    width = k.shape[-1]
    length = seq // dilation
    n_res = max(1, min(dilation, ATTN_MAX_RESIDUES_PER_STEP, ATTN_ROWS_PER_STEP // length))
    spec = pl.BlockSpec((None, n_res, length, width), lambda b, r: (b, r, 0, 0))
    return pl.pallas_call(
        _attn_kernel,
        grid=(batch, dilation // n_res),
        in_specs=[spec, spec, spec],
        out_specs=[spec, pl.BlockSpec((None, n_res, length, LANES), lambda b, r: (b, r, 0, 0))],
        out_shape=[
            jax.ShapeDtypeStruct((batch, dilation, length, width), BF16),
            jax.ShapeDtypeStruct((batch, dilation, length, LANES), F32),
        ],
        compiler_params=_cparams(("parallel", "parallel")),
        name=f"dilated_attention_d{dilation}",
    )(q, k, v)


def _split3_bf16(v):
    hi = v.astype(BF16)
    r1 = v - hi.astype(F32)
    mid = r1.astype(BF16)
    lo = (r1 - mid.astype(F32)).astype(BF16)
    return hi, mid, lo


def _attn_out_kernel(x_ref, o0_ref, o1_ref, o2_ref, l0_ref, l1_ref, l2_ref, unperm1_ref, unperm2_ref,
                     expand_ref, wo_ref, out_ref):
    tm, dm = x_ref.shape
    outs = [o0_ref[...].astype(F32)]
    lses = [l0_ref[...]]
    for o_ref, l_ref, unperm_ref in ((o1_ref, l1_ref, unperm1_ref), (o2_ref, l2_ref, unperm2_ref)):
        unperm = unperm_ref[...]
        outs.append(_dot(unperm, o_ref[...].reshape(tm, dm)))
        pieces = _split3_bf16(l_ref[...].reshape(tm, LANES))
        lses.append(_dot(unperm, pieces[0]) + _dot(unperm, pieces[1]) + _dot(unperm, pieces[2]))
    m = jnp.maximum(jnp.maximum(lses[0], lses[1]), lses[2])
    es = [jnp.exp(l - m) for l in lses]
    inv = 1.0 / (es[0] + es[1] + es[2])
    expand = expand_ref[...]
    mix = None
    for e, o in zip(es, outs):
        w = e * inv
        w_hi = w.astype(BF16)
        w_lo = (w - w_hi.astype(F32)).astype(BF16)
        w_full = _dot(w_hi, expand) + _dot(w_lo, expand)
        term = w_full * o
        mix = term if mix is None else mix + term
    out_ref[...] = x_ref[...] + _dot(mix.astype(BF16), wo_ref[...])


def attention_out(x, outs, lses, unperms, expand, w_o, *, batch, seq, tm=512):
    n_tok, dm = x.shape
    tps = seq // tm
    dils = [d for _, d in DILATION_GROUPS]
    row = lambda t: (t, 0)
    const = lambda t: (0, 0)
    o_specs = [pl.BlockSpec((tm, dm), row)] + [_residue_major_spec(tm, d, dm, tps) for d in dils[1:]]
    l_specs = [pl.BlockSpec((tm, LANES), row)] + [_residue_major_spec(tm, d, LANES, tps) for d in dils[1:]]
    o_args = [outs[0].reshape(n_tok, dm)] + [o.reshape(batch, d, tps, tm // d, dm) for o, d in zip(outs[1:], dils[1:])]
    l_args = [lses[0].reshape(n_tok, LANES)] + [l.reshape(batch, d, tps, tm // d, LANES)
                                                for l, d in zip(lses[1:], dils[1:])]
    return pl.pallas_call(
        _attn_out_kernel,
        grid=(n_tok // tm,),
        in_specs=[pl.BlockSpec((tm, dm), row)] + o_specs + l_specs
        + [pl.BlockSpec((tm, tm), const)] * 2 + [pl.BlockSpec((LANES, dm), const), pl.BlockSpec((dm, dm), const)],
        out_specs=pl.BlockSpec((tm, dm), row),
        out_shape=jax.ShapeDtypeStruct(x.shape, x.dtype),
        input_output_aliases={0: 0},
        compiler_params=_cparams(("parallel",)),
        name="attention_out",
    )(x, *o_args, *l_args, *unperms, expand, w_o)


def _pack_kernel(w_ref, o_ref, *, transpose):
    w = w_ref[...]
    if transpose:
        w = w.T
    o_ref[...] = pltpu.bitcast(w.astype(BF16), U32)


def pack_row_pairs(w_layers, layer, *, transpose=False, block=1024):
    _, rows, cols = w_layers.shape
    if transpose:
        out_shape = (cols // 2, rows)
        out_spec = pl.BlockSpec((cols // 2, block), lambda j: (0, j))
    else:
        out_shape = (rows // 2, cols)
        out_spec = pl.BlockSpec((block // 2, cols), lambda j: (j, 0))
    return pl.pallas_call(
        functools.partial(_pack_kernel, transpose=transpose),
        grid=(rows // block,),
        in_specs=[pl.BlockSpec((None, block, cols), lambda j: (layer, j, 0))],
        out_specs=out_spec,
        out_shape=jax.ShapeDtypeStruct(out_shape, U32),
        compiler_params=_cparams(("parallel",)),
        name="pack_weights_t" if transpose else "pack_weights",
    )(w_layers)


def _rope_tables(positions):
    half = ROPE_DIMS // 2
    inv_freq = 1.0 / (ROPE_THETA ** (jnp.arange(0, ROPE_DIMS, 2, dtype=F32) / ROPE_DIMS))
    ang = positions.astype(F32).reshape(-1, 1) * inv_freq
    cos, sin = jnp.cos(ang), jnp.sin(ang)
    pad = HEAD_DIM - ROPE_DIMS
    cosf = jnp.concatenate([cos, cos, jnp.ones((cos.shape[0], pad), F32)], axis=-1)
    sinf = jnp.concatenate([-sin, sin, jnp.zeros((sin.shape[0], pad), F32)], axis=-1)
    assert half * 2 == ROPE_DIMS
    return cosf, sinf


def _residue_perm(tm, dilation):
    m = tm // dilation
    idx = jnp.arange(tm)
    src_row = (idx % m) * dilation + idx // m
    return (src_row[:, None] == jnp.arange(tm)[None, :]).astype(BF16)


def kernel(x, p, positions, a_norm, a_w_in, a_conv, a_w_out, kv_norm, w_kv, k_norm, b_norm, b_w_q, q_norm, b_w_o, ffn_norm, peer_w_q, peer_sub_keys, peer_u, peer_v, ple_norm, ple_w_gate, ple_w_proj):
    batch, seq, dm = x.shape
    depth = p.shape[0]
    n_a = a_norm.shape[0]
    n_tok = batch * seq
    tm = 512
    xf = x.reshape(n_tok, dm)
    cosf, sinf = _rope_tables(positions)
    head_of_lane = jnp.arange(dm) // HEAD_DIM
    expand = (jnp.arange(LANES)[:, None] == head_of_lane[None, :]).astype(BF16)
    dils = [d for _, d in DILATION_GROUPS]
    assert dils[0] == 1 and all(w // d == ATTN_BLOCK for w, d in DILATION_GROUPS)
    perms = [_residue_perm(tm, d) for d in dils[1:]]
    half = ROPE_DIMS // 2
    lane = jnp.arange(HEAD_DIM)
    src_lane = jnp.where(lane < half, lane + half, lane - half)
    rot_mat = ((jnp.arange(HEAD_DIM)[:, None] == src_lane[None, :]) & (lane < ROPE_DIMS)[None, :]).astype(BF16)
    unperms = [pm.T for pm in perms]

    def residue_major(a, d):
        return a.reshape(batch, d, seq // d, a.shape[-1])

    p_flat = p.reshape(depth, n_tok, -1)
    kvs = None
    for i in range(depth):
        if i < n_a:
            xf = conv_mixer(xf, a_norm[i][None], a_w_in[i].astype(BF16), a_conv[i], a_w_out[i].astype(BF16), seq=seq,
                            in_place=i > 0)
        else:
            j = i - n_a
            qs = query_proj(xf, b_norm[j][None], b_w_q[j].astype(BF16), q_norm[j][None], cosf, sinf, rot_mat, perms,
                            batch=batch, seq=seq, tm=tm)
            outs, lses = [], []
            for q_g, (k_g, v_g), d in zip(qs, kvs, dils):
                o_g, lse_g = dilated_attention(residue_major(q_g, d), residue_major(k_g, d), residue_major(v_g, d),
                                               batch=batch, seq=seq, dilation=d)
                outs.append(o_g)
                lses.append(lse_g)
            xf = attention_out(xf, outs, lses, unperms, expand, b_w_o[j].astype(BF16), batch=batch, seq=seq, tm=tm)
        ht, kap, alp, rho, bet = peer_route(xf, ffn_norm[i][None], peer_w_q[i].T.astype(BF16),
                                            peer_sub_keys[i].astype(BF16))
        xf = peer_dense(xf, ht, pack_row_pairs(peer_u, i), pack_row_pairs(peer_v, i, transpose=True),
                        kap, alp, rho, bet, p_flat, i, ple_norm[i][None], ple_w_gate[i].astype(BF16),
                        ple_w_proj[i].astype(BF16))
        if i == n_a - 1:
            kvs = shared_kv(xf, kv_norm[None], w_kv.astype(BF16), k_norm[None], cosf, sinf, rot_mat, perms,
                            batch=batch, seq=seq, tm=tm)
    return xf.reshape(batch, seq, dm)
```

```python
import functools

import jax
import jax.numpy as jnp
from jax import lax
from jax.experimental import pallas as pl
from jax.experimental.pallas import tpu as pltpu

F32 = jnp.float32
BF16 = jnp.bfloat16
U32 = jnp.uint32

LANES = 128
SUBLANES = 8
VMEM_LIMIT_BYTES = 56 * 1024 * 1024

NORM_EPS = 1e-6
HEAD_DIM = 128
KV_HEADS = 8
N_GROUPS = 3
ROPE_DIMS = HEAD_DIM // 4
ROPE_THETA = 500000.0
DILATION_GROUPS = ((128, 1), (512, 4), (2048, 16))
PEER_HEADS = 8
PEER_N_KEYS = 128
PEER_TOPK = 16
INV_SQRT2 = 0.7071067811865476
NEG_INF = float("-inf")


def _cparams(sem):
    return pltpu.CompilerParams(dimension_semantics=sem, vmem_limit_bytes=VMEM_LIMIT_BYTES)


def _rms(x, gain):
    ms = jnp.mean(x * x, axis=-1, keepdims=True)
    return x * lax.rsqrt(ms + NORM_EPS) * gain


def _dot(a, b):
    return jnp.dot(a, b, preferred_element_type=F32)


def _conv_mixer_kernel(x_ref, gain_ref, win_ref, conv_ref, wout_ref, o_ref, gbuf_ref):
    j = pl.program_id(1)
    tm, dm = x_ref.shape
    x = x_ref[...]
    h = _rms(x, gain_ref[...]).astype(BF16)
    bcu = _dot(h, win_ref[...])
    b_gate = bcu[:, :dm]
    g = bcu[:, dm:2 * dm] * bcu[:, 2 * dm:]

    @pl.when(j == 0)
    def _():
        gbuf_ref[0:SUBLANES, :] = jnp.zeros((SUBLANES, dm), F32)

    gbuf_ref[SUBLANES:SUBLANES + tm, :] = g
    g1 = gbuf_ref[SUBLANES - 1:SUBLANES - 1 + tm, :]
    g2 = gbuf_ref[SUBLANES - 2:SUBLANES - 2 + tm, :]
    w = conv_ref[...]
    z = w[0:1, :] * g2 + w[1:2, :] * g1 + w[2:3, :] * g
    gbuf_ref[0:SUBLANES, :] = g[tm - SUBLANES:, :]
    y = _dot((b_gate * z).astype(BF16), wout_ref[...])
    o_ref[...] = x + y


def conv_mixer(x, gain, w_in, conv_w, w_out, *, seq, in_place, tm=512):
    n_tok, dm = x.shape
    tiles_per_seq = seq // tm
    row = lambda b, j: (b * tiles_per_seq + j, 0)
    const = lambda b, j: (0, 0)
    return pl.pallas_call(
        _conv_mixer_kernel,
        grid=(n_tok // seq, tiles_per_seq),
        in_specs=[
            pl.BlockSpec((tm, dm), row),
            pl.BlockSpec((1, dm), const),
            pl.BlockSpec((dm, 3 * dm), const),
            pl.BlockSpec(conv_w.shape, const),
            pl.BlockSpec((dm, dm), const),
        ],
        out_specs=pl.BlockSpec((tm, dm), row),
        out_shape=jax.ShapeDtypeStruct(x.shape, x.dtype),
        scratch_shapes=[pltpu.VMEM((tm + SUBLANES, dm), F32)],
        input_output_aliases={0: 0} if in_place else {},
        compiler_params=_cparams(("arbitrary", "arbitrary")),
        name="conv_mixer",
    )(x, gain, w_in, conv_w, w_out)


def _staircase():
    return [(k1, k2) for k1 in range(PEER_TOPK) for k2 in range(PEER_TOPK) if (k1 + 1) * (k2 + 1) <= PEER_TOPK]


def _tree_max(vals):
    vals = list(vals)
    while len(vals) > 1:
        nxt = [jnp.maximum(vals[a], vals[a + 1]) for a in range(0, len(vals) - 1, 2)]
        if len(vals) % 2:
            nxt.append(vals[-1])
        vals = nxt
    return vals[0]


def _dup_bf16_bits(v):
    bits = lax.bitcast_convert_type(v, U32)
    return bits | (bits >> 16)


def _sorting_network(n):
    pairs = []
    p = 1
    while p < n:
        k = p
        while k >= 1:
            for j in range(k % p, n - k, 2 * k):
                for i in range(min(k, n - j - k)):
                    if (i + j) // (2 * p) == (i + j + k) // (2 * p):
                        pairs.append((i + j, i + j + k))
            k //= 2
        p *= 2
    return pairs


def _compare_exchange(xs, i, j):
    xs[i], xs[j] = jnp.maximum(xs[i], xs[j]), jnp.minimum(xs[i], xs[j])


def _top_sorted(s3):
    n = PEER_TOPK
    assert s3.shape[0] == n and s3.shape[1] == SUBLANES
    xs = [s3[v] for v in range(n)]
    for i, j in _sorting_network(n):
        _compare_exchange(xs, i, j)
    shift = SUBLANES // 2
    while shift >= 1:
        ys = [pltpu.roll(x, shift, 0) for x in xs]
        xs = [jnp.maximum(xs[k], ys[n - 1 - k]) for k in range(n)]
        d = n // 2
        while d >= 1:
            for k in range(n):
                if k & d == 0:
                    _compare_exchange(xs, k, k + d)
            d //= 2
        shift //= 2
    return xs


def _peer_route_kernel(x_ref, gain_ref, wqt_ref, keys_ref,
                       ht_ref, kap_ref, alp_ref, rho_ref, bet_ref, s1_ref, s2_ref):
    tr = x_ref.shape[0]
    groups = PEER_N_KEYS // SUBLANES
    h = _rms(x_ref[...], gain_ref[...])
    ht = h.T.astype(BF16)
    ht_ref[...] = ht
    qt = _dot(wqt_ref[...], ht).astype(BF16)

    tops = {}
    for hd in range(PEER_HEADS):
        for half in range(2):
            r0 = (hd * 2 + half) * PEER_N_KEYS
            s = _dot(keys_ref[hd, half], qt[r0:r0 + PEER_N_KEYS, :])
            (s1_ref if half == 0 else s2_ref)[hd] = s
            tops[hd, half] = _top_sorted(s.reshape(groups, SUBLANES, tr))

    a_k = [jnp.concatenate([tops[hd, 0][k][0:1, :] for hd in range(PEER_HEADS)], axis=0) for k in range(PEER_TOPK)]
    b_k = [jnp.concatenate([tops[hd, 1][k][0:1, :] for hd in range(PEER_HEADS)], axis=0) for k in range(PEER_TOPK)]
    stairs = _staircase()
    cands = [a_k[k1] + b_k[k2] for k1, k2 in stairs]
    work = cands
    for _ in range(PEER_TOPK - 1):
        m = _tree_max(work)
        work = [jnp.where(c == m, NEG_INF, c) for c in work]
    tau = _tree_max(work)
    top_sum = cands[0]
    z = None
    theta = [None] * PEER_TOPK
    for (k1, k2), c in zip(stairs, cands):
        sel = c >= tau
        e = jnp.where(sel, jnp.exp(c - top_sum), 0.0)
        z = e if z is None else z + e
        t = jnp.where(sel, a_k[k1], jnp.inf)
        theta[k2] = t if theta[k2] is None else jnp.minimum(theta[k2], t)
    inv_z = 0.5 / z
    n_multi = 1 + max(k2 for k1, k2 in stairs if k1 >= 1)
    n_columns = None
    for t in theta:
        hit = jnp.where(t < jnp.inf, 1.0, 0.0)
        n_columns = hit if n_columns is None else n_columns + hit

    for hd in range(PEER_HEADS):
        s1 = s1_ref[hd].reshape(groups, SUBLANES, tr)
        s2 = s2_ref[hd].reshape(groups, SUBLANES, tr)
        row = lambda v: jnp.broadcast_to(v[hd:hd + 1, :], (SUBLANES, tr))[None]
        kap = jnp.zeros(s1.shape, F32)
        for k2 in range(n_multi):
            kap = jnp.where(s1 >= row(theta[k2]), float(k2 + 1), kap)
        kap = jnp.where(s1 >= tops[hd, 0][0][None], row(n_columns), kap)
        rho = jnp.full(s2.shape, float(PEER_TOPK), F32)
        for k in reversed(range(PEER_TOPK)):
            rho = jnp.where(s2 >= tops[hd, 1][k][None], float(k), rho)
        alp = jnp.exp(s1 - tops[hd, 0][0][None]).astype(BF16).astype(F32)
        bet = jnp.exp(s2 - tops[hd, 1][0][None]) * row(inv_z)
        kap_ref[hd] = _dup_bf16_bits(kap).reshape(PEER_N_KEYS, tr)
        alp_ref[hd] = _dup_bf16_bits(alp).reshape(PEER_N_KEYS, tr)
        rho_ref[hd] = rho.reshape(PEER_N_KEYS, tr).astype(BF16)
        bet_ref[hd] = bet.reshape(PEER_N_KEYS, tr).astype(BF16)


def peer_route(x, gain, wq_t, sub_keys, *, tr=256):
    n_tok, dm = x.shape
    nq = wq_t.shape[0]
    route_spec = pl.BlockSpec((PEER_HEADS, PEER_N_KEYS, tr), lambda t: (0, 0, t))
    route_shape = (PEER_HEADS, PEER_N_KEYS, n_tok)
    return pl.pallas_call(
        _peer_route_kernel,
        grid=(n_tok // tr,),
        in_specs=[
            pl.BlockSpec((tr, dm), lambda t: (t, 0)),
            pl.BlockSpec((1, dm), lambda t: (0, 0)),
            pl.BlockSpec((nq, dm), lambda t: (0, 0)),
            pl.BlockSpec(sub_keys.shape, lambda t: (0, 0, 0, 0)),
        ],
        out_specs=[pl.BlockSpec((dm, tr), lambda t: (0, t)), route_spec, route_spec, route_spec, route_spec],
        out_shape=[
            jax.ShapeDtypeStruct((dm, n_tok), BF16),
            jax.ShapeDtypeStruct(route_shape, U32),
            jax.ShapeDtypeStruct(route_shape, U32),
            jax.ShapeDtypeStruct(route_shape, BF16),
            jax.ShapeDtypeStruct(route_shape, BF16),
        ],
        scratch_shapes=[pltpu.VMEM((PEER_HEADS, PEER_N_KEYS, tr), F32)] * 2,
        compiler_params=_cparams(("parallel",)),
        name="peer_route",
    )(x, gain, wq_t, sub_keys)


def _ple(x, p, gain, w_gate, w_proj):
    h = _rms(x, gain).astype(BF16)
    return x + jax.nn.sigmoid(_dot(h, w_gate)) * _dot(p.astype(BF16), w_proj)


def _peer_dense_kernel(x_ref, ht_ref, u_ref, vt_ref, kap_ref, alp_ref, rho_in_ref, bet_in_ref,
                       p_ref, ple_gain_ref, wg_ref, wp_ref,
                       o_ref, acc_ref, rho_ref, bet_ref, *, sub_experts, lookahead):
    c = pl.program_id(1)
    te = vt_ref.shape[1]
    tm = ht_ref.shape[1]
    n_sub = te // sub_experts
    blocks_per_sub = sub_experts // PEER_N_KEYS
    blocks_per_chunk = te // PEER_N_KEYS
    assert blocks_per_chunk % SUBLANES == 0

    @pl.when(c == 0)
    def _():
        acc_ref[...] = jnp.zeros_like(acc_ref)
        rho_ref[...] = rho_in_ref[...]
        bet_ref[...] = bet_in_ref[...]

    ht = ht_ref[...]
    row0 = pl.multiple_of(c * blocks_per_chunk, SUBLANES)

    def expert_act(s):
        u = pltpu.bitcast(u_ref[pl.ds(s * sub_experts // 2, sub_experts // 2), :], BF16)
        return _dot(u, ht)

    def gated(s, act):
        act = act * (1.0 + lax.erf(act * INV_SQRT2))
        actb = act.astype(BF16)
        parts = []
        for b in range(blocks_per_sub):
            r = s * blocks_per_sub + b
            tile0 = row0 + (r // SUBLANES) * SUBLANES
            rr = r % SUBLANES
            gate = None
            for hd in range(PEER_HEADS):
                kap8 = kap_ref[hd, pl.ds(tile0, SUBLANES), :]
                alp8 = alp_ref[hd, pl.ds(tile0, SUBLANES), :]
                kap = pltpu.bitcast(jnp.broadcast_to(kap8[rr:rr + 1, :], (PEER_N_KEYS // 2, tm)), BF16)
                alp = pltpu.bitcast(jnp.broadcast_to(alp8[rr:rr + 1, :], (PEER_N_KEYS // 2, tm)), BF16)
                term = jnp.where(rho_ref[hd] < kap, alp * bet_ref[hd], jnp.zeros((), BF16))
                gate = term if gate is None else gate + term
            parts.append(gate * actb[b * PEER_N_KEYS:(b + 1) * PEER_N_KEYS, :])
        return jnp.concatenate(parts, axis=0) if len(parts) > 1 else parts[0]

    acts = {s: expert_act(s) for s in range(min(lookahead, n_sub))}
    for s in range(n_sub):
        if s + lookahead < n_sub:
            acts[s + lookahead] = expert_act(s + lookahead)
        p = gated(s, acts.pop(s))
        acc_ref[...] += _dot(pltpu.bitcast(vt_ref[:, pl.ds(s * sub_experts, sub_experts)], BF16), p)

    @pl.when(c == pl.num_programs(1) - 1)
    def _():
        x = x_ref[...] + acc_ref[...].T
        o_ref[...] = _ple(x, p_ref[...], ple_gain_ref[...], wg_ref[...], wp_ref[...])


def peer_dense(x, ht, u, vt, kap, alp, rho, bet, p_layers, layer, ple_gain, w_gate, w_proj, *,
               tm=512, te=2048, sub_experts=512, lookahead=3):
    n_tok, dm = x.shape
    n_exp = vt.shape[1]
    dp = p_layers.shape[-1]
    route_spec = pl.BlockSpec((PEER_HEADS, PEER_N_KEYS, tm), lambda t, c: (0, 0, t))
    const = lambda t, c: (0, 0)
    return pl.pallas_call(
        functools.partial(_peer_dense_kernel, sub_experts=sub_experts, lookahead=lookahead),
        grid=(n_tok // tm, n_exp // te),
        in_specs=[
            pl.BlockSpec((tm, dm), lambda t, c: (t, 0)),
            pl.BlockSpec((dm, tm), lambda t, c: (0, t)),
            pl.BlockSpec((te // 2, dm), lambda t, c: (c, 0)),
            pl.BlockSpec((dm // 2, te), lambda t, c: (0, c)),
            route_spec, route_spec, route_spec, route_spec,
            pl.BlockSpec((None, tm, dp), lambda t, c: (layer, t, 0)),
            pl.BlockSpec((1, dm), const),
            pl.BlockSpec((dm, dm), const),
            pl.BlockSpec((dp, dm), const),
        ],
        out_specs=pl.BlockSpec((tm, dm), lambda t, c: (t, 0)),
        out_shape=jax.ShapeDtypeStruct(x.shape, x.dtype),
        scratch_shapes=[
            pltpu.VMEM((dm, tm), F32),
            pltpu.VMEM((PEER_HEADS, PEER_N_KEYS, tm), BF16),
            pltpu.VMEM((PEER_HEADS, PEER_N_KEYS, tm), BF16),
        ],
        input_output_aliases={0: 0},
        compiler_params=_cparams(("parallel", "arbitrary")),
        name="peer_dense",
    )(x, ht, u, vt, kap, alp, rho, bet, p_layers, ple_gain, w_gate, w_proj)


def _head_norm_rope(y, gain, cosf, sinf, rot_mat):
    yn = _rms(y, gain)
    return yn * cosf + _dot(yn.astype(BF16), rot_mat) * sinf


def _store_residue_major(perm_ref, y, out_ref):
    d, m, w = out_ref.shape
    out_ref[...] = _dot(perm_ref[...], y).astype(out_ref.dtype).reshape(d, m, w)


def _kv_kernel(x_ref, gain_ref, w_ref, kgain_ref, cos_ref, sin_ref, rot_ref, perm1_ref, perm2_ref,
               k0_ref, v0_ref, k1_ref, v1_ref, k2_ref, v2_ref):
    dm = x_ref.shape[1]
    h = _rms(x_ref[...], gain_ref[...]).astype(BF16)
    kv = _dot(h, w_ref[...])
    v0_ref[...] = kv[:, dm:].astype(BF16)
    cosf, sinf, kg, rot = cos_ref[...], sin_ref[...], kgain_ref[...], rot_ref[...]
    for hd in range(KV_HEADS):
        sl = slice(hd * HEAD_DIM, (hd + 1) * HEAD_DIM)
        k0_ref[:, sl] = _head_norm_rope(kv[:, sl], kg, cosf, sinf, rot).astype(BF16)
    for perm_ref, k_ref, v_ref in ((perm1_ref, k1_ref, v1_ref), (perm2_ref, k2_ref, v2_ref)):
        _store_residue_major(perm_ref, k0_ref[...], k_ref)
        _store_residue_major(perm_ref, v0_ref[...], v_ref)


def _residue_major_spec(tm, dilation, width, tiles_per_seq):
    return pl.BlockSpec((None, dilation, None, tm // dilation, width),
                        lambda t: (t // tiles_per_seq, 0, t % tiles_per_seq, 0, 0))


def _residue_major_shape(batch, seq, tm, dilation, width, dtype):
    return jax.ShapeDtypeStruct((batch, dilation, seq // tm, tm // dilation, width), dtype)


def shared_kv(x, gain, w_kv, k_gain, cosf, sinf, rot_mat, perms, *, batch, seq, tm=512):
    n_tok, dm = x.shape
    tps = seq // tm
    row = pl.BlockSpec((tm, dm), lambda t: (t, 0))
    dils = [d for _, d in DILATION_GROUPS[1:]]
    const2 = lambda t: (0, 0)
    outs = pl.pallas_call(
        _kv_kernel,
        grid=(n_tok // tm,),
        in_specs=[
            row,
            pl.BlockSpec((1, dm), const2),
            pl.BlockSpec((dm, 2 * dm), const2),
            pl.BlockSpec((1, HEAD_DIM), const2),
            pl.BlockSpec((tm, HEAD_DIM), lambda t: (t, 0)),
            pl.BlockSpec((tm, HEAD_DIM), lambda t: (t, 0)),
            pl.BlockSpec((HEAD_DIM, HEAD_DIM), const2),
            pl.BlockSpec((tm, tm), const2),
            pl.BlockSpec((tm, tm), const2),
        ],
        out_specs=[row, row] + [_residue_major_spec(tm, d, dm, tps) for d in dils for _ in range(2)],
        out_shape=[jax.ShapeDtypeStruct((n_tok, dm), BF16)] * 2
        + [_residue_major_shape(batch, seq, tm, d, dm, BF16) for d in dils for _ in range(2)],
        compiler_params=_cparams(("parallel",)),
        name="shared_kv",
    )(x, gain, w_kv, k_gain, cosf, sinf, rot_mat, *perms)
    return [(outs[0], outs[1]), (outs[2], outs[3]), (outs[4], outs[5])]


def _q_kernel(x_ref, gain_ref, w_ref, qgain_ref, cos_ref, sin_ref, rot_ref, perm1_ref, perm2_ref,
              q0_ref, q1_ref, q2_ref, qbuf_ref):
    dm = x_ref.shape[1]
    h = _rms(x_ref[...], gain_ref[...]).astype(BF16)
    q = _dot(h, w_ref[...])
    cosf, sinf, qg, rot = cos_ref[...], sin_ref[...], qgain_ref[...], rot_ref[...]
    for hd in range(q.shape[1] // HEAD_DIM):
        sl = slice(hd * HEAD_DIM, (hd + 1) * HEAD_DIM)
        y = _head_norm_rope(q[:, sl], qg, cosf, sinf, rot).astype(BF16)
        if hd < KV_HEADS:
            q0_ref[:, sl] = y
        else:
            qbuf_ref[:, hd * HEAD_DIM - dm:(hd + 1) * HEAD_DIM - dm] = y
    _store_residue_major(perm1_ref, qbuf_ref[:, :dm], q1_ref)
    _store_residue_major(perm2_ref, qbuf_ref[:, dm:], q2_ref)


def query_proj(x, gain, w_q, q_gain, cosf, sinf, rot_mat, perms, *, batch, seq, tm=512):
    n_tok, dm = x.shape
    nq = w_q.shape[1]
    tps = seq // tm
    dils = [d for _, d in DILATION_GROUPS[1:]]
    const2 = lambda t: (0, 0)
    return pl.pallas_call(
        _q_kernel,
        grid=(n_tok // tm,),
        in_specs=[
            pl.BlockSpec((tm, dm), lambda t: (t, 0)),
            pl.BlockSpec((1, dm), const2),
            pl.BlockSpec((dm, nq), const2),
            pl.BlockSpec((1, HEAD_DIM), const2),
            pl.BlockSpec((tm, HEAD_DIM), lambda t: (t, 0)),
            pl.BlockSpec((tm, HEAD_DIM), lambda t: (t, 0)),
            pl.BlockSpec((HEAD_DIM, HEAD_DIM), const2),
            pl.BlockSpec((tm, tm), const2),
            pl.BlockSpec((tm, tm), const2),
        ],
        out_specs=[pl.BlockSpec((tm, dm), lambda t: (t, 0))] + [_residue_major_spec(tm, d, dm, tps) for d in dils],
        out_shape=[jax.ShapeDtypeStruct((n_tok, dm), BF16)]
        + [_residue_major_shape(batch, seq, tm, d, dm, BF16) for d in dils],
        scratch_shapes=[pltpu.VMEM((tm, nq - dm), BF16)],
        compiler_params=_cparams(("parallel",)),
        name="query_proj",
    )(x, gain, w_q, q_gain, cosf, sinf, rot_mat, *perms)


ATTN_BLOCK = 128
ATTN_ROWS_PER_STEP = 1024
ATTN_MAX_RESIDUES_PER_STEP = 4


def _attn_scores(q, k):
    return lax.dot_general(q, k, (((1,), (1,)), ((), ())), preferred_element_type=F32) * (HEAD_DIM ** -0.5)


def _attn_softmax(s, mask):
    s = jnp.where(mask, s, NEG_INF)
    m = jnp.max(s, axis=-1, keepdims=True)
    e = jnp.exp(s - m)
    den = jnp.sum(e, axis=-1, keepdims=True)
    return (e / den).astype(BF16), m + jnp.log(den)


def _attn_kernel(q_ref, k_ref, v_ref, o_ref, lse_ref):
    n_res, length = q_ref.shape[0], q_ref.shape[1]
    blk = ATTN_BLOCK
    n_blk = length // blk
    qi = lax.broadcasted_iota(jnp.int32, (blk, blk), 0)
    ki = lax.broadcasted_iota(jnp.int32, (blk, blk), 1)
    mask_first = ki <= qi
    qi2 = lax.broadcasted_iota(jnp.int32, (blk, 2 * blk), 0)
    ki2 = lax.broadcasted_iota(jnp.int32, (blk, 2 * blk), 1)
    dist = qi2 + blk - ki2
    mask_band = (dist >= 0) & (dist <= blk)
    lane = lax.broadcasted_iota(jnp.int32, (blk, LANES), 1)

    def one_block(q_rows, k_rows, mask):
        heads = [slice(hd * HEAD_DIM, (hd + 1) * HEAD_DIM) for hd in range(KV_HEADS)]
        scores = [[_attn_scores(q_ref[r, q_rows, sl], k_ref[r, k_rows, sl]) for sl in heads] for r in range(n_res)]
        probs = [[_attn_softmax(s, mask) for s in row] for row in scores]
        for r in range(n_res):
            lse_tile = jnp.zeros((blk, LANES), F32)
            for hd, sl in enumerate(heads):
                p, lse = probs[r][hd]
                o_ref[r, q_rows, sl] = _dot(p, v_ref[r, k_rows, sl]).astype(o_ref.dtype)
                lse_tile = jnp.where(lane == hd, lse, lse_tile)
            lse_ref[r, q_rows, :] = lse_tile

    one_block(pl.ds(0, blk), pl.ds(0, blk), mask_first)

    def body(n, carry):
        q0 = pl.multiple_of(n * blk, blk)
        k0 = pl.multiple_of((n - 1) * blk, blk)
        one_block(pl.ds(q0, blk), pl.ds(k0, 2 * blk), mask_band)
        return carry

    lax.fori_loop(1, n_blk, body, 0)


def dilated_attention(q, k, v, *, batch, seq, dilation):
    width = k.shape[-1]
    length = seq // dilation
    n_res = max(1, min(dilation, ATTN_MAX_RESIDUES_PER_STEP, ATTN_ROWS_PER_STEP // length))
    spec = pl.BlockSpec((None, n_res, length, width), lambda b, r: (b, r, 0, 0))
    return pl.pallas_call(
        _attn_kernel,
        grid=(batch, dilation // n_res),
        in_specs=[spec, spec, spec],
        out_specs=[spec, pl.BlockSpec((None, n_res, length, LANES), lambda b, r: (b, r, 0, 0))],
        out_shape=[
            jax.ShapeDtypeStruct((batch, dilation, length, width), BF16),
            jax.ShapeDtypeStruct((batch, dilation, length, LANES), F32),
        ],
        compiler_params=_cparams(("parallel", "parallel")),
        name=f"dilated_attention_d{dilation}",
    )(q, k, v)


def _split3_bf16(v):
    hi = v.astype(BF16)
    r1 = v - hi.astype(F32)
    mid = r1.astype(BF16)
    lo = (r1 - mid.astype(F32)).astype(BF16)
    return hi, mid, lo


def _attn_out_kernel(x_ref, o0_ref, o1_ref, o2_ref, l0_ref, l1_ref, l2_ref, unperm1_ref, unperm2_ref,
                     expand_ref, wo_ref, out_ref):
    tm, dm = x_ref.shape
    outs = [o0_ref[...].astype(F32)]
    lses = [l0_ref[...]]
    for o_ref, l_ref, unperm_ref in ((o1_ref, l1_ref, unperm1_ref), (o2_ref, l2_ref, unperm2_ref)):
        unperm = unperm_ref[...]
        outs.append(_dot(unperm, o_ref[...].reshape(tm, dm)))
        pieces = _split3_bf16(l_ref[...].reshape(tm, LANES))
        lses.append(_dot(unperm, pieces[0]) + _dot(unperm, pieces[1]) + _dot(unperm, pieces[2]))
    m = jnp.maximum(jnp.maximum(lses[0], lses[1]), lses[2])
    es = [jnp.exp(l - m) for l in lses]
    inv = 1.0 / (es[0] + es[1] + es[2])
    expand = expand_ref[...]
    mix = None
    for e, o in zip(es, outs):
        w = e * inv
        w_hi = w.astype(BF16)
        w_lo = (w - w_hi.astype(F32)).astype(BF16)
        w_full = _dot(w_hi, expand) + _dot(w_lo, expand)
        term = w_full * o
        mix = term if mix is None else mix + term
    out_ref[...] = x_ref[...] + _dot(mix.astype(BF16), wo_ref[...])


def attention_out(x, outs, lses, unperms, expand, w_o, *, batch, seq, tm=512):
    n_tok, dm = x.shape
    tps = seq // tm
    dils = [d for _, d in DILATION_GROUPS]
    row = lambda t: (t, 0)
    const = lambda t: (0, 0)
    o_specs = [pl.BlockSpec((tm, dm), row)] + [_residue_major_spec(tm, d, dm, tps) for d in dils[1:]]
    l_specs = [pl.BlockSpec((tm, LANES), row)] + [_residue_major_spec(tm, d, LANES, tps) for d in dils[1:]]
    o_args = [outs[0].reshape(n_tok, dm)] + [o.reshape(batch, d, tps, tm // d, dm) for o, d in zip(outs[1:], dils[1:])]
    l_args = [lses[0].reshape(n_tok, LANES)] + [l.reshape(batch, d, tps, tm // d, LANES)
                                                for l, d in zip(lses[1:], dils[1:])]
    return pl.pallas_call(
        _attn_out_kernel,
        grid=(n_tok // tm,),
        in_specs=[pl.BlockSpec((tm, dm), row)] + o_specs + l_specs
        + [pl.BlockSpec((tm, tm), const)] * 2 + [pl.BlockSpec((LANES, dm), const), pl.BlockSpec((dm, dm), const)],
        out_specs=pl.BlockSpec((tm, dm), row),
        out_shape=jax.ShapeDtypeStruct(x.shape, x.dtype),
        input_output_aliases={0: 0},
        compiler_params=_cparams(("parallel",)),
        name="attention_out",
    )(x, *o_args, *l_args, *unperms, expand, w_o)


def _pack_kernel(w_ref, o_ref, *, transpose):
    w = w_ref[...]
    if transpose:
        w = w.T
    o_ref[...] = pltpu.bitcast(w.astype(BF16), U32)


def pack_row_pairs(w_layers, layer, *, transpose=False, block=1024):
    _, rows, cols = w_layers.shape
    if transpose:
        out_shape = (cols // 2, rows)
        out_spec = pl.BlockSpec((cols // 2, block), lambda j: (0, j))
    else:
        out_shape = (rows // 2, cols)
        out_spec = pl.BlockSpec((block // 2, cols), lambda j: (j, 0))
    return pl.pallas_call(
        functools.partial(_pack_kernel, transpose=transpose),
        grid=(rows // block,),
        in_specs=[pl.BlockSpec((None, block, cols), lambda j: (layer, j, 0))],
        out_specs=out_spec,
        out_shape=jax.ShapeDtypeStruct(out_shape, U32),
        compiler_params=_cparams(("parallel",)),
        name="pack_weights_t" if transpose else "pack_weights",
    )(w_layers)


def _rope_tables(positions):
    half = ROPE_DIMS // 2
    inv_freq = 1.0 / (ROPE_THETA ** (jnp.arange(0, ROPE_DIMS, 2, dtype=F32) / ROPE_DIMS))
    ang = positions.astype(F32).reshape(-1, 1) * inv_freq
    cos, sin = jnp.cos(ang), jnp.sin(ang)
    pad = HEAD_DIM - ROPE_DIMS
    cosf = jnp.concatenate([cos, cos, jnp.ones((cos.shape[0], pad), F32)], axis=-1)
    sinf = jnp.concatenate([-sin, sin, jnp.zeros((sin.shape[0], pad), F32)], axis=-1)
    assert half * 2 == ROPE_DIMS
    return cosf, sinf


def _residue_perm(tm, dilation):
    m = tm // dilation
    idx = jnp.arange(tm)
    src_row = (idx % m) * dilation + idx // m
    return (src_row[:, None] == jnp.arange(tm)[None, :]).astype(BF16)


def kernel(x, p, positions, a_norm, a_w_in, a_conv, a_w_out, kv_norm, w_kv, k_norm, b_norm, b_w_q, q_norm, b_w_o, ffn_norm, peer_w_q, peer_sub_keys, peer_u, peer_v, ple_norm, ple_w_gate, ple_w_proj):
    batch, seq, dm = x.shape
    depth = p.shape[0]
    n_a = a_norm.shape[0]
    n_tok = batch * seq
    tm = 512
    xf = x.reshape(n_tok, dm)
    cosf, sinf = _rope_tables(positions)
    head_of_lane = jnp.arange(dm) // HEAD_DIM
    expand = (jnp.arange(LANES)[:, None] == head_of_lane[None, :]).astype(BF16)
    dils = [d for _, d in DILATION_GROUPS]
    assert dils[0] == 1 and all(w // d == ATTN_BLOCK for w, d in DILATION_GROUPS)
    perms = [_residue_perm(tm, d) for d in dils[1:]]
    half = ROPE_DIMS // 2
    lane = jnp.arange(HEAD_DIM)
    src_lane = jnp.where(lane < half, lane + half, lane - half)
    rot_mat = ((jnp.arange(HEAD_DIM)[:, None] == src_lane[None, :]) & (lane < ROPE_DIMS)[None, :]).astype(BF16)
    unperms = [pm.T for pm in perms]

    def residue_major(a, d):
        return a.reshape(batch, d, seq // d, a.shape[-1])

    p_flat = p.reshape(depth, n_tok, -1)
    kvs = None
    for i in range(depth):
        if i < n_a:
            xf = conv_mixer(xf, a_norm[i][None], a_w_in[i].astype(BF16), a_conv[i], a_w_out[i].astype(BF16), seq=seq,
                            in_place=i > 0)
        else:
            j = i - n_a
            qs = query_proj(xf, b_norm[j][None], b_w_q[j].astype(BF16), q_norm[j][None], cosf, sinf, rot_mat, perms,
                            batch=batch, seq=seq, tm=tm)
            outs, lses = [], []
            for q_g, (k_g, v_g), d in zip(qs, kvs, dils):
                o_g, lse_g = dilated_attention(residue_major(q_g, d), residue_major(k_g, d), residue_major(v_g, d),
                                               batch=batch, seq=seq, dilation=d)
                outs.append(o_g)
                lses.append(lse_g)
            xf = attention_out(xf, outs, lses, unperms, expand, b_w_o[j].astype(BF16), batch=batch, seq=seq, tm=tm)
        ht, kap, alp, rho, bet = peer_route(xf, ffn_norm[i][None], peer_w_q[i].T.astype(BF16),
                                            peer_sub_keys[i].astype(BF16))
        xf = peer_dense(xf, ht, pack_row_pairs(peer_u, i), pack_row_pairs(peer_v, i, transpose=True),
                        kap, alp, rho, bet, p_flat, i, ple_norm[i][None], ple_w_gate[i].astype(BF16),
                        ple_w_proj[i].astype(BF16))
        if i == n_a - 1:
            kvs = shared_kv(xf, kv_norm[None], w_kv.astype(BF16), k_norm[None], cosf, sinf, rot_mat, perms,
                            batch=batch, seq=seq, tm=tm)
    return xf.reshape(batch, seq, dm)
```

```python
import functools

import jax
import jax.numpy as jnp
from jax import lax
from jax.experimental import pallas as pl
from jax.experimental.pallas import tpu as pltpu

F32 = jnp.float32
BF16 = jnp.bfloat16
U32 = jnp.uint32

LANES = 128
SUBLANES = 8
VMEM_LIMIT_BYTES = 56 * 1024 * 1024

NORM_EPS = 1e-6
HEAD_DIM = 128
KV_HEADS = 8
N_GROUPS = 3
ROPE_DIMS = HEAD_DIM // 4
ROPE_THETA = 500000.0
DILATION_GROUPS = ((128, 1), (512, 4), (2048, 16))
PEER_HEADS = 8
PEER_N_KEYS = 128
PEER_TOPK = 16
INV_SQRT2 = 0.7071067811865476
NEG_INF = float("-inf")


def _cparams(sem):
    return pltpu.CompilerParams(dimension_semantics=sem, vmem_limit_bytes=VMEM_LIMIT_BYTES)


def _rms(x, gain):
    ms = jnp.mean(x * x, axis=-1, keepdims=True)
    return x * lax.rsqrt(ms + NORM_EPS) * gain


def _dot(a, b):
    return jnp.dot(a, b, preferred_element_type=F32)


def _conv_mixer_kernel(x_ref, gain_ref, win_ref, conv_ref, wout_ref, o_ref, gbuf_ref):
    j = pl.program_id(1)
    tm, dm = x_ref.shape
    x = x_ref[...]
    h = _rms(x, gain_ref[...]).astype(BF16)
    bcu = _dot(h, win_ref[...])
    b_gate = bcu[:, :dm]
    g = bcu[:, dm:2 * dm] * bcu[:, 2 * dm:]

    @pl.when(j == 0)
    def _():
        gbuf_ref[0:SUBLANES, :] = jnp.zeros((SUBLANES, dm), F32)

    gbuf_ref[SUBLANES:SUBLANES + tm, :] = g
    g1 = gbuf_ref[SUBLANES - 1:SUBLANES - 1 + tm, :]
    g2 = gbuf_ref[SUBLANES - 2:SUBLANES - 2 + tm, :]
    w = conv_ref[...]
    z = w[0:1, :] * g2 + w[1:2, :] * g1 + w[2:3, :] * g
    gbuf_ref[0:SUBLANES, :] = g[tm - SUBLANES:, :]
    y = _dot((b_gate * z).astype(BF16), wout_ref[...])
    o_ref[...] = x + y


def conv_mixer(x, gain, w_in, conv_w, w_out, *, seq, in_place, tm=512):
    n_tok, dm = x.shape
    tiles_per_seq = seq // tm
    row = lambda b, j: (b * tiles_per_seq + j, 0)
    const = lambda b, j: (0, 0)
    return pl.pallas_call(
        _conv_mixer_kernel,
        grid=(n_tok // seq, tiles_per_seq),
        in_specs=[
            pl.BlockSpec((tm, dm), row),
            pl.BlockSpec((1, dm), const),
            pl.BlockSpec((dm, 3 * dm), const),
            pl.BlockSpec(conv_w.shape, const),
            pl.BlockSpec((dm, dm), const),
        ],
        out_specs=pl.BlockSpec((tm, dm), row),
        out_shape=jax.ShapeDtypeStruct(x.shape, x.dtype),
        scratch_shapes=[pltpu.VMEM((tm + SUBLANES, dm), F32)],
        input_output_aliases={0: 0} if in_place else {},
        compiler_params=_cparams(("arbitrary", "arbitrary")),
        name="conv_mixer",
    )(x, gain, w_in, conv_w, w_out)


def _staircase():
    return [(k1, k2) for k1 in range(PEER_TOPK) for k2 in range(PEER_TOPK) if (k1 + 1) * (k2 + 1) <= PEER_TOPK]


def _dup_bf16_bits(v):
    bits = lax.bitcast_convert_type(v, U32)
    return bits | (bits >> 16)


def _sorting_network(n):
    pairs = []
    p = 1
    while p < n:
        k = p
        while k >= 1:
            for j in range(k % p, n - k, 2 * k):
                for i in range(min(k, n - j - k)):
                    if (i + j) // (2 * p) == (i + j + k) // (2 * p):
                        pairs.append((i + j, i + j + k))
            k //= 2
        p *= 2
    return pairs


def _compare_exchange(xs, i, j):
    xs[i], xs[j] = jnp.maximum(xs[i], xs[j]), jnp.minimum(xs[i], xs[j])


def _top_sorted(s3):
    n = PEER_TOPK
    assert s3.shape[0] == n and s3.shape[1] == SUBLANES
    xs = [s3[v] for v in range(n)]
    for i, j in _sorting_network(n):
        _compare_exchange(xs, i, j)
    shift = SUBLANES // 2
    while shift >= 1:
        ys = [pltpu.roll(x, shift, 0) for x in xs]
        xs = [jnp.maximum(xs[k], ys[n - 1 - k]) for k in range(n)]
        d = n // 2
        while d >= 1:
            for k in range(n):
                if k & d == 0:
                    _compare_exchange(xs, k, k + d)
            d //= 2
        shift //= 2
    return xs


def _peer_route_kernel(x_ref, gain_ref, wqt_ref, keys_ref,
                       ht_ref, kap_ref, alp_ref, rho_ref, bet_ref, s1_ref, s2_ref):
    tr = x_ref.shape[0]
    groups = PEER_N_KEYS // SUBLANES
    h = _rms(x_ref[...], gain_ref[...])
    ht = h.T.astype(BF16)
    ht_ref[...] = ht
    qt = _dot(wqt_ref[...], ht).astype(BF16)

    tops = {}
    for hd in range(PEER_HEADS):
        for half in range(2):
            r0 = (hd * 2 + half) * PEER_N_KEYS
            s = _dot(keys_ref[hd, half], qt[r0:r0 + PEER_N_KEYS, :])
            (s1_ref if half == 0 else s2_ref)[hd] = s
            tops[hd, half] = _top_sorted(s.reshape(groups, SUBLANES, tr))

    a_k = [jnp.concatenate([tops[hd, 0][k][0:1, :] for hd in range(PEER_HEADS)], axis=0) for k in range(PEER_TOPK)]
    b_k = [jnp.concatenate([tops[hd, 1][k][0:1, :] for hd in range(PEER_HEADS)], axis=0) for k in range(PEER_TOPK)]
    stairs = _staircase()
    cands = [a_k[k1] + b_k[k2] for k1, k2 in stairs]
    best = [c for (k1, _), c in zip(stairs, cands) if k1 == 0]
    assert len(best) == PEER_TOPK
    for row_k1 in range(1, PEER_TOPK):
        row = [c for (k1, _), c in zip(stairs, cands) if k1 == row_k1]
        for j, c in enumerate(row):
            best[PEER_TOPK - 1 - j] = jnp.maximum(best[PEER_TOPK - 1 - j], c)
        d = PEER_TOPK // 2
        while d >= 1:
            for k in range(PEER_TOPK):
                if k & d == 0:
                    _compare_exchange(best, k, k + d)
            d //= 2
    tau = best[PEER_TOPK - 1]
    top_sum = cands[0]
    z = None
    theta = [None] * PEER_TOPK
    for (k1, k2), c in zip(stairs, cands):
        sel = c >= tau
        e = jnp.where(sel, jnp.exp(c - top_sum), 0.0)
        z = e if z is None else z + e
        t = jnp.where(sel, a_k[k1], jnp.inf)
        theta[k2] = t if theta[k2] is None else jnp.minimum(theta[k2], t)
    inv_z = 0.5 / z
    n_multi = 1 + max(k2 for k1, k2 in stairs if k1 >= 1)
    n_columns = None
    for t in theta:
        hit = jnp.where(t < jnp.inf, 1.0, 0.0)
        n_columns = hit if n_columns is None else n_columns + hit

    for hd in range(PEER_HEADS):
        s1 = s1_ref[hd].reshape(groups, SUBLANES, tr)
        s2 = s2_ref[hd].reshape(groups, SUBLANES, tr)
        row = lambda v: jnp.broadcast_to(v[hd:hd + 1, :], (SUBLANES, tr))[None]
        kap = jnp.zeros(s1.shape, F32)
        for k2 in range(n_multi):
            kap = jnp.where(s1 >= row(theta[k2]), float(k2 + 1), kap)
        kap = jnp.where(s1 >= tops[hd, 0][0][None], row(n_columns), kap)
        rho = jnp.full(s2.shape, float(PEER_TOPK), F32)
        for k in reversed(range(PEER_TOPK)):
            rho = jnp.where(s2 >= tops[hd, 1][k][None], float(k), rho)
        alp = jnp.exp(s1 - tops[hd, 0][0][None]).astype(BF16).astype(F32)
        bet = jnp.exp(s2 - tops[hd, 1][0][None]) * row(inv_z)
        kap_ref[hd] = _dup_bf16_bits(kap).reshape(PEER_N_KEYS, tr)
        alp_ref[hd] = _dup_bf16_bits(alp).reshape(PEER_N_KEYS, tr)
        rho_ref[hd] = rho.reshape(PEER_N_KEYS, tr).astype(BF16)
        bet_ref[hd] = bet.reshape(PEER_N_KEYS, tr).astype(BF16)


def peer_route(x, gain, wq_t, sub_keys, *, tr=256):
    n_tok, dm = x.shape
    nq = wq_t.shape[0]
    route_spec = pl.BlockSpec((PEER_HEADS, PEER_N_KEYS, tr), lambda t: (0, 0, t))
    route_shape = (PEER_HEADS, PEER_N_KEYS, n_tok)
    return pl.pallas_call(
        _peer_route_kernel,
        grid=(n_tok // tr,),
        in_specs=[
            pl.BlockSpec((tr, dm), lambda t: (t, 0)),
            pl.BlockSpec((1, dm), lambda t: (0, 0)),
            pl.BlockSpec((nq, dm), lambda t: (0, 0)),
            pl.BlockSpec(sub_keys.shape, lambda t: (0, 0, 0, 0)),
        ],
        out_specs=[pl.BlockSpec((dm, tr), lambda t: (0, t)), route_spec, route_spec, route_spec, route_spec],
        out_shape=[
            jax.ShapeDtypeStruct((dm, n_tok), BF16),
            jax.ShapeDtypeStruct(route_shape, U32),
            jax.ShapeDtypeStruct(route_shape, U32),
            jax.ShapeDtypeStruct(route_shape, BF16),
            jax.ShapeDtypeStruct(route_shape, BF16),
        ],
        scratch_shapes=[pltpu.VMEM((PEER_HEADS, PEER_N_KEYS, tr), F32)] * 2,
        compiler_params=_cparams(("parallel",)),
        name="peer_route",
    )(x, gain, wq_t, sub_keys)


def _ple(x, p, gain, w_gate, w_proj):
    h = _rms(x, gain).astype(BF16)
    return x + jax.nn.sigmoid(_dot(h, w_gate)) * _dot(p.astype(BF16), w_proj)


def _peer_dense_kernel(x_ref, ht_ref, u_ref, vt_ref, kap_ref, alp_ref, rho_in_ref, bet_in_ref,
                       p_ref, ple_gain_ref, wg_ref, wp_ref,
                       o_ref, acc_ref, rho_ref, bet_ref, *, sub_experts, lookahead):
    c = pl.program_id(1)
    te = vt_ref.shape[1]
    tm = ht_ref.shape[1]
    n_sub = te // sub_experts
    blocks_per_sub = sub_experts // PEER_N_KEYS
    blocks_per_chunk = te // PEER_N_KEYS
    assert blocks_per_chunk % SUBLANES == 0

    @pl.when(c == 0)
    def _():
        acc_ref[...] = jnp.zeros_like(acc_ref)
        rho_ref[...] = rho_in_ref[...]
        bet_ref[...] = bet_in_ref[...]

    ht = ht_ref[...]
    row0 = pl.multiple_of(c * blocks_per_chunk, SUBLANES)

    def expert_act(s):
        u = pltpu.bitcast(u_ref[pl.ds(s * sub_experts // 2, sub_experts // 2), :], BF16)
        return _dot(u, ht)

    def gated(s, act):
        act = act * (1.0 + lax.erf(act * INV_SQRT2))
        actb = act.astype(BF16)
        parts = []
        for b in range(blocks_per_sub):
            r = s * blocks_per_sub + b
            tile0 = row0 + (r // SUBLANES) * SUBLANES
            rr = r % SUBLANES
            gate = None
            for hd in range(PEER_HEADS):
                kap8 = kap_ref[hd, pl.ds(tile0, SUBLANES), :]
                alp8 = alp_ref[hd, pl.ds(tile0, SUBLANES), :]
                kap = pltpu.bitcast(jnp.broadcast_to(kap8[rr:rr + 1, :], (PEER_N_KEYS // 2, tm)), BF16)
                alp = pltpu.bitcast(jnp.broadcast_to(alp8[rr:rr + 1, :], (PEER_N_KEYS // 2, tm)), BF16)
                term = jnp.where(rho_ref[hd] < kap, alp * bet_ref[hd], jnp.zeros((), BF16))
                gate = term if gate is None else gate + term
            parts.append(gate * actb[b * PEER_N_KEYS:(b + 1) * PEER_N_KEYS, :])
        return jnp.concatenate(parts, axis=0) if len(parts) > 1 else parts[0]

    acts = {s: expert_act(s) for s in range(min(lookahead, n_sub))}
    for s in range(n_sub):
        if s + lookahead < n_sub:
            acts[s + lookahead] = expert_act(s + lookahead)
        p = gated(s, acts.pop(s))
        acc_ref[...] += _dot(pltpu.bitcast(vt_ref[:, pl.ds(s * sub_experts, sub_experts)], BF16), p)

    @pl.when(c == pl.num_programs(1) - 1)
    def _():
        x = x_ref[...] + acc_ref[...].T
        o_ref[...] = _ple(x, p_ref[...], ple_gain_ref[...], wg_ref[...], wp_ref[...])


def peer_dense(x, ht, u, vt, kap, alp, rho, bet, p_layers, layer, ple_gain, w_gate, w_proj, *,
               tm=512, te=2048, sub_experts=512, lookahead=3):
    n_tok, dm = x.shape
    n_exp = vt.shape[1]
    dp = p_layers.shape[-1]
    route_spec = pl.BlockSpec((PEER_HEADS, PEER_N_KEYS, tm), lambda t, c: (0, 0, t))
    const = lambda t, c: (0, 0)
    return pl.pallas_call(
        functools.partial(_peer_dense_kernel, sub_experts=sub_experts, lookahead=lookahead),
        grid=(n_tok // tm, n_exp // te),
        in_specs=[
            pl.BlockSpec((tm, dm), lambda t, c: (t, 0)),
            pl.BlockSpec((dm, tm), lambda t, c: (0, t)),
            pl.BlockSpec((te // 2, dm), lambda t, c: (c, 0)),
            pl.BlockSpec((dm // 2, te), lambda t, c: (0, c)),
            route_spec, route_spec, route_spec, route_spec,
            pl.BlockSpec((None, tm, dp), lambda t, c: (layer, t, 0)),
            pl.BlockSpec((1, dm), const),
            pl.BlockSpec((dm, dm), const),
            pl.BlockSpec((dp, dm), const),
        ],
        out_specs=pl.BlockSpec((tm, dm), lambda t, c: (t, 0)),
        out_shape=jax.ShapeDtypeStruct(x.shape, x.dtype),
        scratch_shapes=[
            pltpu.VMEM((dm, tm), F32),
            pltpu.VMEM((PEER_HEADS, PEER_N_KEYS, tm), BF16),
            pltpu.VMEM((PEER_HEADS, PEER_N_KEYS, tm), BF16),
        ],
        input_output_aliases={0: 0},
        compiler_params=_cparams(("parallel", "arbitrary")),
        name="peer_dense",
    )(x, ht, u, vt, kap, alp, rho, bet, p_layers, ple_gain, w_gate, w_proj)


def _head_norm_rope(y, gain, cosf, sinf, rot_mat):
    yn = _rms(y, gain)
    return yn * cosf + _dot(yn.astype(BF16), rot_mat) * sinf


def _store_residue_major(perm_ref, y, out_ref):
    d, m, w = out_ref.shape
    out_ref[...] = _dot(perm_ref[...], y).astype(out_ref.dtype).reshape(d, m, w)


def _kv_kernel(x_ref, gain_ref, w_ref, kgain_ref, cos_ref, sin_ref, rot_ref, perm1_ref, perm2_ref,
               k0_ref, v0_ref, k1_ref, v1_ref, k2_ref, v2_ref):
    dm = x_ref.shape[1]
    h = _rms(x_ref[...], gain_ref[...]).astype(BF16)
    kv = _dot(h, w_ref[...])
    v0_ref[...] = kv[:, dm:].astype(BF16)
    cosf, sinf, kg, rot = cos_ref[...], sin_ref[...], kgain_ref[...], rot_ref[...]
    for hd in range(KV_HEADS):
        sl = slice(hd * HEAD_DIM, (hd + 1) * HEAD_DIM)
        k0_ref[:, sl] = _head_norm_rope(kv[:, sl], kg, cosf, sinf, rot).astype(BF16)
    for perm_ref, k_ref, v_ref in ((perm1_ref, k1_ref, v1_ref), (perm2_ref, k2_ref, v2_ref)):
        _store_residue_major(perm_ref, k0_ref[...], k_ref)
        _store_residue_major(perm_ref, v0_ref[...], v_ref)


def _residue_major_spec(tm, dilation, width, tiles_per_seq):
    return pl.BlockSpec((None, dilation, None, tm // dilation, width),
                        lambda t: (t // tiles_per_seq, 0, t % tiles_per_seq, 0, 0))


def _residue_major_shape(batch, seq, tm, dilation, width, dtype):
    return jax.ShapeDtypeStruct((batch, dilation, seq // tm, tm // dilation, width), dtype)


def shared_kv(x, gain, w_kv, k_gain, cosf, sinf, rot_mat, perms, *, batch, seq, tm=512):
    n_tok, dm = x.shape
    tps = seq // tm
    row = pl.BlockSpec((tm, dm), lambda t: (t, 0))
    dils = [d for _, d in DILATION_GROUPS[1:]]
    const2 = lambda t: (0, 0)
    outs = pl.pallas_call(
        _kv_kernel,
        grid=(n_tok // tm,),
        in_specs=[
            row,
            pl.BlockSpec((1, dm), const2),
            pl.BlockSpec((dm, 2 * dm), const2),
            pl.BlockSpec((1, HEAD_DIM), const2),
            pl.BlockSpec((tm, HEAD_DIM), lambda t: (t, 0)),
            pl.BlockSpec((tm, HEAD_DIM), lambda t: (t, 0)),
            pl.BlockSpec((HEAD_DIM, HEAD_DIM), const2),
            pl.BlockSpec((tm, tm), const2),
            pl.BlockSpec((tm, tm), const2),
        ],
        out_specs=[row, row] + [_residue_major_spec(tm, d, dm, tps) for d in dils for _ in range(2)],
        out_shape=[jax.ShapeDtypeStruct((n_tok, dm), BF16)] * 2
        + [_residue_major_shape(batch, seq, tm, d, dm, BF16) for d in dils for _ in range(2)],
        compiler_params=_cparams(("parallel",)),
        name="shared_kv",
    )(x, gain, w_kv, k_gain, cosf, sinf, rot_mat, *perms)
    return [(outs[0], outs[1]), (outs[2], outs[3]), (outs[4], outs[5])]


def _q_kernel(x_ref, gain_ref, w_ref, qgain_ref, cos_ref, sin_ref, rot_ref, perm1_ref, perm2_ref,
              q0_ref, q1_ref, q2_ref, qbuf_ref):
    dm = x_ref.shape[1]
    h = _rms(x_ref[...], gain_ref[...]).astype(BF16)
    q = _dot(h, w_ref[...])
    cosf, sinf, qg, rot = cos_ref[...], sin_ref[...], qgain_ref[...], rot_ref[...]
    for hd in range(q.shape[1] // HEAD_DIM):
        sl = slice(hd * HEAD_DIM, (hd + 1) * HEAD_DIM)
        y = _head_norm_rope(q[:, sl], qg, cosf, sinf, rot).astype(BF16)
        if hd < KV_HEADS:
            q0_ref[:, sl] = y
        else:
            qbuf_ref[:, hd * HEAD_DIM - dm:(hd + 1) * HEAD_DIM - dm] = y
    _store_residue_major(perm1_ref, qbuf_ref[:, :dm], q1_ref)
    _store_residue_major(perm2_ref, qbuf_ref[:, dm:], q2_ref)


def query_proj(x, gain, w_q, q_gain, cosf, sinf, rot_mat, perms, *, batch, seq, tm=512):
    n_tok, dm = x.shape
    nq = w_q.shape[1]
    tps = seq // tm
    dils = [d for _, d in DILATION_GROUPS[1:]]
    const2 = lambda t: (0, 0)
    return pl.pallas_call(
        _q_kernel,
        grid=(n_tok // tm,),
        in_specs=[
            pl.BlockSpec((tm, dm), lambda t: (t, 0)),
            pl.BlockSpec((1, dm), const2),
            pl.BlockSpec((dm, nq), const2),
            pl.BlockSpec((1, HEAD_DIM), const2),
            pl.BlockSpec((tm, HEAD_DIM), lambda t: (t, 0)),
            pl.BlockSpec((tm, HEAD_DIM), lambda t: (t, 0)),
            pl.BlockSpec((HEAD_DIM, HEAD_DIM), const2),
            pl.BlockSpec((tm, tm), const2),
            pl.BlockSpec((tm, tm), const2),
        ],
        out_specs=[pl.BlockSpec((tm, dm), lambda t: (t, 0))] + [_residue_major_spec(tm, d, dm, tps) for d in dils],
        out_shape=[jax.ShapeDtypeStruct((n_tok, dm), BF16)]
        + [_residue_major_shape(batch, seq, tm, d, dm, BF16) for d in dils],
        scratch_shapes=[pltpu.VMEM((tm, nq - dm), BF16)],
        compiler_params=_cparams(("parallel",)),
        name="query_proj",
    )(x, gain, w_q, q_gain, cosf, sinf, rot_mat, *perms)


ATTN_BLOCK = 128
ATTN_ROWS_PER_STEP = 1024
ATTN_MAX_RESIDUES_PER_STEP = 4


def _attn_scores(q, k):
    return lax.dot_general(q, k, (((1,), (1,)), ((), ())), preferred_element_type=F32) * (HEAD_DIM ** -0.5)


def _attn_softmax(s, mask):
    s = jnp.where(mask, s, NEG_INF)
    m = jnp.max(s, axis=-1, keepdims=True)
    e = jnp.exp(s - m)
    den = jnp.sum(e, axis=-1, keepdims=True)
    return (e / den).astype(BF16), m + jnp.log(den)


def _attn_kernel(q_ref, k_ref, v_ref, o_ref, lse_ref):
    n_res, length = q_ref.shape[0], q_ref.shape[1]
    blk = ATTN_BLOCK
    n_blk = length // blk
    qi = lax.broadcasted_iota(jnp.int32, (blk, blk), 0)
    ki = lax.broadcasted_iota(jnp.int32, (blk, blk), 1)
    mask_first = ki <= qi
    qi2 = lax.broadcasted_iota(jnp.int32, (blk, 2 * blk), 0)
    ki2 = lax.broadcasted_iota(jnp.int32, (blk, 2 * blk), 1)
    dist = qi2 + blk - ki2
    mask_band = (dist >= 0) & (dist <= blk)
    lane = lax.broadcasted_iota(jnp.int32, (blk, LANES), 1)

    def one_block(q_rows, k_rows, mask):
        heads = [slice(hd * HEAD_DIM, (hd + 1) * HEAD_DIM) for hd in range(KV_HEADS)]
        scores = [[_attn_scores(q_ref[r, q_rows, sl], k_ref[r, k_rows, sl]) for sl in heads] for r in range(n_res)]
        probs = [[_attn_softmax(s, mask) for s in row] for row in scores]
        for r in range(n_res):
            lse_tile = jnp.zeros((blk, LANES), F32)
            for hd, sl in enumerate(heads):
                p, lse = probs[r][hd]
                o_ref[r, q_rows, sl] = _dot(p, v_ref[r, k_rows, sl]).astype(o_ref.dtype)
                lse_tile = jnp.where(lane == hd, lse, lse_tile)
            lse_ref[r, q_rows, :] = lse_tile

    one_block(pl.ds(0, blk), pl.ds(0, blk), mask_first)

    def body(n, carry):
        q0 = pl.multiple_of(n * blk, blk)
        k0 = pl.multiple_of((n - 1) * blk, blk)
        one_block(pl.ds(q0, blk), pl.ds(k0, 2 * blk), mask_band)
        return carry

    lax.fori_loop(1, n_blk, body, 0)


def dilated_attention(q, k, v, *, batch, seq, dilation):
    width = k.shape[-1]
    length = seq // dilation
    n_res = max(1, min(dilation, ATTN_MAX_RESIDUES_PER_STEP, ATTN_ROWS_PER_STEP // length))
    spec = pl.BlockSpec((None, n_res, length, width), lambda b, r: (b, r, 0, 0))
    return pl.pallas_call(
        _attn_kernel,
        grid=(batch, dilation // n_res),
        in_specs=[spec, spec, spec],
        out_specs=[spec, pl.BlockSpec((None, n_res, length, LANES), lambda b, r: (b, r, 0, 0))],
        out_shape=[
            jax.ShapeDtypeStruct((batch, dilation, length, width), BF16),
            jax.ShapeDtypeStruct((batch, dilation, length, LANES), F32),
        ],
        compiler_params=_cparams(("parallel", "parallel")),
        name=f"dilated_attention_d{dilation}",
    )(q, k, v)


def _split3_bf16(v):
    hi = v.astype(BF16)
    r1 = v - hi.astype(F32)
    mid = r1.astype(BF16)
    lo = (r1 - mid.astype(F32)).astype(BF16)
    return hi, mid, lo


def _attn_out_kernel(x_ref, o0_ref, o1_ref, o2_ref, l0_ref, l1_ref, l2_ref, unperm1_ref, unperm2_ref,
                     expand_ref, wo_ref, out_ref):
    tm, dm = x_ref.shape
    outs = [o0_ref[...].astype(F32)]
    lses = [l0_ref[...]]
    for o_ref, l_ref, unperm_ref in ((o1_ref, l1_ref, unperm1_ref), (o2_ref, l2_ref, unperm2_ref)):
        unperm = unperm_ref[...]
        outs.append(_dot(unperm, o_ref[...].reshape(tm, dm)))
        pieces = _split3_bf16(l_ref[...].reshape(tm, LANES))
        lses.append(_dot(unperm, pieces[0]) + _dot(unperm, pieces[1]) + _dot(unperm, pieces[2]))
    m = jnp.maximum(jnp.maximum(lses[0], lses[1]), lses[2])
    es = [jnp.exp(l - m) for l in lses]
    inv = 1.0 / (es[0] + es[1] + es[2])
    expand = expand_ref[...]
    mix = None
    for e, o in zip(es, outs):
        w = e * inv
        w_hi = w.astype(BF16)
        w_lo = (w - w_hi.astype(F32)).astype(BF16)
        w_full = _dot(w_hi, expand) + _dot(w_lo, expand)
        term = w_full * o
        mix = term if mix is None else mix + term
    out_ref[...] = x_ref[...] + _dot(mix.astype(BF16), wo_ref[...])


def attention_out(x, outs, lses, unperms, expand, w_o, *, batch, seq, tm=512):
    n_tok, dm = x.shape
    tps = seq // tm
    dils = [d for _, d in DILATION_GROUPS]
    row = lambda t: (t, 0)
    const = lambda t: (0, 0)
    o_specs = [pl.BlockSpec((tm, dm), row)] + [_residue_major_spec(tm, d, dm, tps) for d in dils[1:]]
    l_specs = [pl.BlockSpec((tm, LANES), row)] + [_residue_major_spec(tm, d, LANES, tps) for d in dils[1:]]
    o_args = [outs[0].reshape(n_tok, dm)] + [o.reshape(batch, d, tps, tm // d, dm) for o, d in zip(outs[1:], dils[1:])]
    l_args = [lses[0].reshape(n_tok, LANES)] + [l.reshape(batch, d, tps, tm // d, LANES)
                                                for l, d in zip(lses[1:], dils[1:])]
    return pl.pallas_call(
        _attn_out_kernel,
        grid=(n_tok // tm,),
        in_specs=[pl.BlockSpec((tm, dm), row)] + o_specs + l_specs
        + [pl.BlockSpec((tm, tm), const)] * 2 + [pl.BlockSpec((LANES, dm), const), pl.BlockSpec((dm, dm), const)],
        out_specs=pl.BlockSpec((tm, dm), row),
        out_shape=jax.ShapeDtypeStruct(x.shape, x.dtype),
        input_output_aliases={0: 0},
        compiler_params=_cparams(("parallel",)),
        name="attention_out",
    )(x, *o_args, *l_args, *unperms, expand, w_o)


def _pack_kernel(w_ref, o_ref, *, transpose):
    w = w_ref[...]
    if transpose:
        w = w.T
    o_ref[...] = pltpu.bitcast(w.astype(BF16), U32)


def pack_row_pairs(w_layers, layer, *, transpose=False, block=1024):
    _, rows, cols = w_layers.shape
    if transpose:
        out_shape = (cols // 2, rows)
        out_spec = pl.BlockSpec((cols // 2, block), lambda j: (0, j))
    else:
        out_shape = (rows // 2, cols)
        out_spec = pl.BlockSpec((block // 2, cols), lambda j: (j, 0))
    return pl.pallas_call(
        functools.partial(_pack_kernel, transpose=transpose),
        grid=(rows // block,),
        in_specs=[pl.BlockSpec((None, block, cols), lambda j: (layer, j, 0))],
        out_specs=out_spec,
        out_shape=jax.ShapeDtypeStruct(out_shape, U32),
        compiler_params=_cparams(("parallel",)),
        name="pack_weights_t" if transpose else "pack_weights",
    )(w_layers)


def _rope_tables(positions):
    half = ROPE_DIMS // 2
    inv_freq = 1.0 / (ROPE_THETA ** (jnp.arange(0, ROPE_DIMS, 2, dtype=F32) / ROPE_DIMS))
    ang = positions.astype(F32).reshape(-1, 1) * inv_freq
    cos, sin = jnp.cos(ang), jnp.sin(ang)
    pad = HEAD_DIM - ROPE_DIMS
    cosf = jnp.concatenate([cos, cos, jnp.ones((cos.shape[0], pad), F32)], axis=-1)
    sinf = jnp.concatenate([-sin, sin, jnp.zeros((sin.shape[0], pad), F32)], axis=-1)
    assert half * 2 == ROPE_DIMS
    return cosf, sinf


def _residue_perm(tm, dilation):
    m = tm // dilation
    idx = jnp.arange(tm)
    src_row = (idx % m) * dilation + idx // m
    return (src_row[:, None] == jnp.arange(tm)[None, :]).astype(BF16)


def kernel(x, p, positions, a_norm, a_w_in, a_conv, a_w_out, kv_norm, w_kv, k_norm, b_norm, b_w_q, q_norm, b_w_o, ffn_norm, peer_w_q, peer_sub_keys, peer_u, peer_v, ple_norm, ple_w_gate, ple_w_proj):
    batch, seq, dm = x.shape
    depth = p.shape[0]
    n_a = a_norm.shape[0]
    n_tok = batch * seq
    tm = 512
    xf = x.reshape(n_tok, dm)
    cosf, sinf = _rope_tables(positions)
    head_of_lane = jnp.arange(dm) // HEAD_DIM
    expand = (jnp.arange(LANES)[:, None] == head_of_lane[None, :]).astype(BF16)
    dils = [d for _, d in DILATION_GROUPS]
    assert dils[0] == 1 and all(w // d == ATTN_BLOCK for w, d in DILATION_GROUPS)
    perms = [_residue_perm(tm, d) for d in dils[1:]]
    half = ROPE_DIMS // 2
    lane = jnp.arange(HEAD_DIM)
    src_lane = jnp.where(lane < half, lane + half, lane - half)
    rot_mat = ((jnp.arange(HEAD_DIM)[:, None] == src_lane[None, :]) & (lane < ROPE_DIMS)[None, :]).astype(BF16)
    unperms = [pm.T for pm in perms]

    def residue_major(a, d):
        return a.reshape(batch, d, seq // d, a.shape[-1])

    p_flat = p.reshape(depth, n_tok, -1)
    kvs = None
    for i in range(depth):
        if i < n_a:
            xf = conv_mixer(xf, a_norm[i][None], a_w_in[i].astype(BF16), a_conv[i], a_w_out[i].astype(BF16), seq=seq,
                            in_place=i > 0)
        else:
            j = i - n_a
            qs = query_proj(xf, b_norm[j][None], b_w_q[j].astype(BF16), q_norm[j][None], cosf, sinf, rot_mat, perms,
                            batch=batch, seq=seq, tm=tm)
            outs, lses = [], []
            for q_g, (k_g, v_g), d in zip(qs, kvs, dils):
                o_g, lse_g = dilated_attention(residue_major(q_g, d), residue_major(k_g, d), residue_major(v_g, d),
                                               batch=batch, seq=seq, dilation=d)
                outs.append(o_g)
                lses.append(lse_g)
            xf = attention_out(xf, outs, lses, unperms, expand, b_w_o[j].astype(BF16), batch=batch, seq=seq, tm=tm)
        ht, kap, alp, rho, bet = peer_route(xf, ffn_norm[i][None], peer_w_q[i].T.astype(BF16),
                                            peer_sub_keys[i].astype(BF16))
        xf = peer_dense(xf, ht, pack_row_pairs(peer_u, i), pack_row_pairs(peer_v, i, transpose=True),
                        kap, alp, rho, bet, p_flat, i, ple_norm[i][None], ple_w_gate[i].astype(BF16),
                        ple_w_proj[i].astype(BF16))
        if i == n_a - 1:
            kvs = shared_kv(xf, kv_norm[None], w_kv.astype(BF16), k_norm[None], cosf, sinf, rot_mat, perms,
                            batch=batch, seq=seq, tm=tm)
    return xf.reshape(batch, seq, dm)
```

```python
import functools

import jax
import jax.numpy as jnp
from jax import lax
from jax.experimental import pallas as pl
from jax.experimental.pallas import tpu as pltpu

F32 = jnp.float32
BF16 = jnp.bfloat16
U32 = jnp.uint32

LANES = 128
SUBLANES = 8
VMEM_LIMIT_BYTES = 56 * 1024 * 1024

NORM_EPS = 1e-6
HEAD_DIM = 128
KV_HEADS = 8
N_GROUPS = 3
ROPE_DIMS = HEAD_DIM // 4
ROPE_THETA = 500000.0
DILATION_GROUPS = ((128, 1), (512, 4), (2048, 16))
PEER_HEADS = 8
PEER_N_KEYS = 128
PEER_TOPK = 16
INV_SQRT2 = 0.7071067811865476
NEG_INF = float("-inf")


def _cparams(sem):
    return pltpu.CompilerParams(dimension_semantics=sem, vmem_limit_bytes=VMEM_LIMIT_BYTES)


def _rms(x, gain):
    ms = jnp.mean(x * x, axis=-1, keepdims=True)
    return x * lax.rsqrt(ms + NORM_EPS) * gain


def _dot(a, b):
    return jnp.dot(a, b, preferred_element_type=F32)


def _conv_mixer_kernel(x_ref, gain_ref, win_ref, conv_ref, wout_ref, o_ref, gbuf_ref):
    j = pl.program_id(1)
    tm, dm = x_ref.shape
    x = x_ref[...]
    h = _rms(x, gain_ref[...]).astype(BF16)
    bcu = _dot(h, win_ref[...])
    b_gate = bcu[:, :dm]
    g = bcu[:, dm:2 * dm] * bcu[:, 2 * dm:]

    @pl.when(j == 0)
    def _():
        gbuf_ref[0:SUBLANES, :] = jnp.zeros((SUBLANES, dm), F32)

    gbuf_ref[SUBLANES:SUBLANES + tm, :] = g
    g1 = gbuf_ref[SUBLANES - 1:SUBLANES - 1 + tm, :]
    g2 = gbuf_ref[SUBLANES - 2:SUBLANES - 2 + tm, :]
    w = conv_ref[...]
    z = w[0:1, :] * g2 + w[1:2, :] * g1 + w[2:3, :] * g
    gbuf_ref[0:SUBLANES, :] = g[tm - SUBLANES:, :]
    y = _dot((b_gate * z).astype(BF16), wout_ref[...])
    o_ref[...] = x + y


def conv_mixer(x, gain, w_in, conv_w, w_out, *, seq, in_place, tm=512):
    n_tok, dm = x.shape
    tiles_per_seq = seq // tm
    row = lambda b, j: (b * tiles_per_seq + j, 0)
    const = lambda b, j: (0, 0)
    return pl.pallas_call(
        _conv_mixer_kernel,
        grid=(n_tok // seq, tiles_per_seq),
        in_specs=[
            pl.BlockSpec((tm, dm), row),
            pl.BlockSpec((1, dm), const),
            pl.BlockSpec((dm, 3 * dm), const),
            pl.BlockSpec(conv_w.shape, const),
            pl.BlockSpec((dm, dm), const),
        ],
        out_specs=pl.BlockSpec((tm, dm), row),
        out_shape=jax.ShapeDtypeStruct(x.shape, x.dtype),
        scratch_shapes=[pltpu.VMEM((tm + SUBLANES, dm), F32)],
        input_output_aliases={0: 0} if in_place else {},
        compiler_params=_cparams(("arbitrary", "arbitrary")),
        name="conv_mixer",
    )(x, gain, w_in, conv_w, w_out)


def _staircase():
    return [(k1, k2) for k1 in range(PEER_TOPK) for k2 in range(PEER_TOPK) if (k1 + 1) * (k2 + 1) <= PEER_TOPK]


def _dup_bf16_bits(v):
    bits = lax.bitcast_convert_type(v, U32)
    return bits | (bits >> 16)


def _sorting_network(n):
    pairs = []
    p = 1
    while p < n:
        k = p
        while k >= 1:
            for j in range(k % p, n - k, 2 * k):
                for i in range(min(k, n - j - k)):
                    if (i + j) // (2 * p) == (i + j + k) // (2 * p):
                        pairs.append((i + j, i + j + k))
            k //= 2
        p *= 2
    return pairs


def _compare_exchange(xs, i, j):
    xs[i], xs[j] = jnp.maximum(xs[i], xs[j]), jnp.minimum(xs[i], xs[j])


def _top_sorted(s3):
    n = PEER_TOPK
    assert s3.shape[0] == n and s3.shape[1] == SUBLANES
    xs = [s3[v] for v in range(n)]
    for i, j in _sorting_network(n):
        _compare_exchange(xs, i, j)
    shift = SUBLANES // 2
    while shift >= 1:
        ys = [pltpu.roll(x, shift, 0) for x in xs]
        xs = [jnp.maximum(xs[k], ys[n - 1 - k]) for k in range(n)]
        d = n // 2
        while d >= 1:
            for k in range(n):
                if k & d == 0:
                    _compare_exchange(xs, k, k + d)
            d //= 2
        shift //= 2
    return xs


def _peer_route_kernel(x_ref, gain_ref, wqt_ref, keys_ref,
                       ht_ref, kap_ref, alp_ref, rho_ref, bet_ref, s1_ref, s2_ref):
    tr = x_ref.shape[0]
    groups = PEER_N_KEYS // SUBLANES
    h = _rms(x_ref[...], gain_ref[...])
    ht = h.T.astype(BF16)
    ht_ref[...] = ht
    qt = _dot(wqt_ref[...], ht).astype(BF16)

    tops = {}
    for hd in range(PEER_HEADS):
        for half in range(2):
            r0 = (hd * 2 + half) * PEER_N_KEYS
            s = _dot(keys_ref[hd, half], qt[r0:r0 + PEER_N_KEYS, :])
            (s1_ref if half == 0 else s2_ref)[hd] = s
            tops[hd, half] = _top_sorted(s.reshape(groups, SUBLANES, tr))

    a_k = [jnp.concatenate([tops[hd, 0][k][0:1, :] for hd in range(PEER_HEADS)], axis=0) for k in range(PEER_TOPK)]
    b_k = [jnp.concatenate([tops[hd, 1][k][0:1, :] for hd in range(PEER_HEADS)], axis=0) for k in range(PEER_TOPK)]
    stairs = _staircase()
    cands = [a_k[k1] + b_k[k2] for k1, k2 in stairs]
    best = [c for (k1, _), c in zip(stairs, cands) if k1 == 0]
    assert len(best) == PEER_TOPK
    for row_k1 in range(1, PEER_TOPK):
        row = [c for (k1, _), c in zip(stairs, cands) if k1 == row_k1]
        for j, c in enumerate(row):
            best[PEER_TOPK - 1 - j] = jnp.maximum(best[PEER_TOPK - 1 - j], c)
        d = PEER_TOPK // 2
        while d >= 1:
            for k in range(PEER_TOPK):
                if k & d == 0:
                    _compare_exchange(best, k, k + d)
            d //= 2
    tau = best[PEER_TOPK - 1]
    top_sum = cands[0]
    z = None
    theta = [None] * PEER_TOPK
    for (k1, k2), c in zip(stairs, cands):
        sel = c >= tau
        e = jnp.where(sel, jnp.exp(c - top_sum), 0.0)
        z = e if z is None else z + e
        t = jnp.where(sel, a_k[k1], jnp.inf)
        theta[k2] = t if theta[k2] is None else jnp.minimum(theta[k2], t)
    inv_z = 0.5 / z
    n_multi = 1 + max(k2 for k1, k2 in stairs if k1 >= 1)
    n_columns = None
    for t in theta:
        hit = jnp.where(t < jnp.inf, 1.0, 0.0)
        n_columns = hit if n_columns is None else n_columns + hit

    for hd in range(PEER_HEADS):
        s1 = s1_ref[hd].reshape(groups, SUBLANES, tr)
        s2 = s2_ref[hd].reshape(groups, SUBLANES, tr)
        row = lambda v: jnp.broadcast_to(v[hd:hd + 1, :], (SUBLANES, tr))[None]
        kap = jnp.zeros(s1.shape, F32)
        for k2 in range(n_multi):
            kap = jnp.where(s1 >= row(theta[k2]), float(k2 + 1), kap)
        kap = jnp.where(s1 >= tops[hd, 0][0][None], row(n_columns), kap)
        rho = jnp.full(s2.shape, float(PEER_TOPK), F32)
        for k in reversed(range(PEER_TOPK)):
            rho = jnp.where(s2 >= tops[hd, 1][k][None], float(k), rho)
        alp = jnp.exp(s1 - tops[hd, 0][0][None]).astype(BF16).astype(F32)
        bet = jnp.exp(s2 - tops[hd, 1][0][None]) * row(inv_z)
        kap_ref[hd] = _dup_bf16_bits(kap).reshape(PEER_N_KEYS, tr)
        alp_ref[hd] = _dup_bf16_bits(alp).reshape(PEER_N_KEYS, tr)
        rho_ref[hd] = rho.reshape(PEER_N_KEYS, tr).astype(BF16)
        bet_ref[hd] = bet.reshape(PEER_N_KEYS, tr).astype(BF16)


def peer_route(x, gain, wq_t, sub_keys, *, tr=512):
    n_tok, dm = x.shape
    nq = wq_t.shape[0]
    route_spec = pl.BlockSpec((PEER_HEADS, PEER_N_KEYS, tr), lambda t: (0, 0, t))
    route_shape = (PEER_HEADS, PEER_N_KEYS, n_tok)
    return pl.pallas_call(
        _peer_route_kernel,
        grid=(n_tok // tr,),
        in_specs=[
            pl.BlockSpec((tr, dm), lambda t: (t, 0)),
            pl.BlockSpec((1, dm), lambda t: (0, 0)),
            pl.BlockSpec((nq, dm), lambda t: (0, 0)),
            pl.BlockSpec(sub_keys.shape, lambda t: (0, 0, 0, 0)),
        ],
        out_specs=[pl.BlockSpec((dm, tr), lambda t: (0, t)), route_spec, route_spec, route_spec, route_spec],
        out_shape=[
            jax.ShapeDtypeStruct((dm, n_tok), BF16),
            jax.ShapeDtypeStruct(route_shape, U32),
            jax.ShapeDtypeStruct(route_shape, U32),
            jax.ShapeDtypeStruct(route_shape, BF16),
            jax.ShapeDtypeStruct(route_shape, BF16),
        ],
        scratch_shapes=[pltpu.VMEM((PEER_HEADS, PEER_N_KEYS, tr), F32)] * 2,
        compiler_params=_cparams(("parallel",)),
        name="peer_route",
    )(x, gain, wq_t, sub_keys)


def _ple(x, p, gain, w_gate, w_proj):
    h = _rms(x, gain).astype(BF16)
    return x + jax.nn.sigmoid(_dot(h, w_gate)) * _dot(p.astype(BF16), w_proj)


def _peer_dense_kernel(x_ref, ht_ref, u_ref, vt_ref, kap_ref, alp_ref, rho_in_ref, bet_in_ref,
                       p_ref, ple_gain_ref, wg_ref, wp_ref,
                       o_ref, acc_ref, rho_ref, bet_ref, *, sub_experts, lookahead):
    c = pl.program_id(1)
    te = vt_ref.shape[1]
    tm = ht_ref.shape[1]
    n_sub = te // sub_experts
    blocks_per_sub = sub_experts // PEER_N_KEYS
    blocks_per_chunk = te // PEER_N_KEYS
    assert blocks_per_chunk % SUBLANES == 0

    @pl.when(c == 0)
    def _():
        acc_ref[...] = jnp.zeros_like(acc_ref)
        rho_ref[...] = rho_in_ref[...]
        bet_ref[...] = bet_in_ref[...]

    ht = ht_ref[...]
    row0 = pl.multiple_of(c * blocks_per_chunk, SUBLANES)

    def expert_act(s):
        u = pltpu.bitcast(u_ref[pl.ds(s * sub_experts // 2, sub_experts // 2), :], BF16)
        return _dot(u, ht)

    def gated(s, act):
        act = act * (1.0 + lax.erf(act * INV_SQRT2))
        actb = act.astype(BF16)
        parts = []
        for b in range(blocks_per_sub):
            r = s * blocks_per_sub + b
            tile0 = row0 + (r // SUBLANES) * SUBLANES
            rr = r % SUBLANES
            gate = None
            for hd in range(PEER_HEADS):
                kap8 = kap_ref[hd, pl.ds(tile0, SUBLANES), :]
                alp8 = alp_ref[hd, pl.ds(tile0, SUBLANES), :]
                kap = pltpu.bitcast(jnp.broadcast_to(kap8[rr:rr + 1, :], (PEER_N_KEYS // 2, tm)), BF16)
                alp = pltpu.bitcast(jnp.broadcast_to(alp8[rr:rr + 1, :], (PEER_N_KEYS // 2, tm)), BF16)
                term = jnp.where(rho_ref[hd] < kap, alp * bet_ref[hd], jnp.zeros((), BF16))
                gate = term if gate is None else gate + term
            parts.append(gate * actb[b * PEER_N_KEYS:(b + 1) * PEER_N_KEYS, :])
        return jnp.concatenate(parts, axis=0) if len(parts) > 1 else parts[0]

    acts = {s: expert_act(s) for s in range(min(lookahead, n_sub))}
    for s in range(n_sub):
        if s + lookahead < n_sub:
            acts[s + lookahead] = expert_act(s + lookahead)
        p = gated(s, acts.pop(s))
        acc_ref[...] += _dot(pltpu.bitcast(vt_ref[:, pl.ds(s * sub_experts, sub_experts)], BF16), p)

    @pl.when(c == pl.num_programs(1) - 1)
    def _():
        x = x_ref[...] + acc_ref[...].T
        o_ref[...] = _ple(x, p_ref[...], ple_gain_ref[...], wg_ref[...], wp_ref[...])


def peer_dense(x, ht, u, vt, kap, alp, rho, bet, p_layers, layer, ple_gain, w_gate, w_proj, *,
               tm=512, te=2048, sub_experts=512, lookahead=3):
    n_tok, dm = x.shape
    n_exp = vt.shape[1]
    dp = p_layers.shape[-1]
    route_spec = pl.BlockSpec((PEER_HEADS, PEER_N_KEYS, tm), lambda t, c: (0, 0, t))
    const = lambda t, c: (0, 0)
    return pl.pallas_call(
        functools.partial(_peer_dense_kernel, sub_experts=sub_experts, lookahead=lookahead),
        grid=(n_tok // tm, n_exp // te),
        in_specs=[
            pl.BlockSpec((tm, dm), lambda t, c: (t, 0)),
            pl.BlockSpec((dm, tm), lambda t, c: (0, t)),
            pl.BlockSpec((te // 2, dm), lambda t, c: (c, 0)),
            pl.BlockSpec((dm // 2, te), lambda t, c: (0, c)),
            route_spec, route_spec, route_spec, route_spec,
            pl.BlockSpec((None, tm, dp), lambda t, c: (layer, t, 0)),
            pl.BlockSpec((1, dm), const),
            pl.BlockSpec((dm, dm), const),
            pl.BlockSpec((dp, dm), const),
        ],
        out_specs=pl.BlockSpec((tm, dm), lambda t, c: (t, 0)),
        out_shape=jax.ShapeDtypeStruct(x.shape, x.dtype),
        scratch_shapes=[
            pltpu.VMEM((dm, tm), F32),
            pltpu.VMEM((PEER_HEADS, PEER_N_KEYS, tm), BF16),
            pltpu.VMEM((PEER_HEADS, PEER_N_KEYS, tm), BF16),
        ],
        input_output_aliases={0: 0},
        compiler_params=_cparams(("parallel", "arbitrary")),
        name="peer_dense",
    )(x, ht, u, vt, kap, alp, rho, bet, p_layers, ple_gain, w_gate, w_proj)


def _head_norm_rope(y, gain, cosf, sinf, rot_mat):
    yn = _rms(y, gain)
    return yn * cosf + _dot(yn.astype(BF16), rot_mat) * sinf


def _store_residue_major(perm_ref, y, out_ref):
    d, m, w = out_ref.shape
    out_ref[...] = _dot(perm_ref[...], y).astype(out_ref.dtype).reshape(d, m, w)


def _kv_kernel(x_ref, gain_ref, w_ref, kgain_ref, cos_ref, sin_ref, rot_ref, perm1_ref, perm2_ref,
               k0_ref, v0_ref, k1_ref, v1_ref, k2_ref, v2_ref):
    dm = x_ref.shape[1]
    h = _rms(x_ref[...], gain_ref[...]).astype(BF16)
    kv = _dot(h, w_ref[...])
    v0_ref[...] = kv[:, dm:].astype(BF16)
    cosf, sinf, kg, rot = cos_ref[...], sin_ref[...], kgain_ref[...], rot_ref[...]
    for hd in range(KV_HEADS):
        sl = slice(hd * HEAD_DIM, (hd + 1) * HEAD_DIM)
        k0_ref[:, sl] = _head_norm_rope(kv[:, sl], kg, cosf, sinf, rot).astype(BF16)
    for perm_ref, k_ref, v_ref in ((perm1_ref, k1_ref, v1_ref), (perm2_ref, k2_ref, v2_ref)):
        _store_residue_major(perm_ref, k0_ref[...], k_ref)
        _store_residue_major(perm_ref, v0_ref[...], v_ref)


def _residue_major_spec(tm, dilation, width, tiles_per_seq):
    return pl.BlockSpec((None, dilation, None, tm // dilation, width),
                        lambda t: (t // tiles_per_seq, 0, t % tiles_per_seq, 0, 0))


def _residue_major_shape(batch, seq, tm, dilation, width, dtype):
    return jax.ShapeDtypeStruct((batch, dilation, seq // tm, tm // dilation, width), dtype)


def shared_kv(x, gain, w_kv, k_gain, cosf, sinf, rot_mat, perms, *, batch, seq, tm=512):
    n_tok, dm = x.shape
    tps = seq // tm
    row = pl.BlockSpec((tm, dm), lambda t: (t, 0))
    dils = [d for _, d in DILATION_GROUPS[1:]]
    const2 = lambda t: (0, 0)
    outs = pl.pallas_call(
        _kv_kernel,
        grid=(n_tok // tm,),
        in_specs=[
            row,
            pl.BlockSpec((1, dm), const2),
            pl.BlockSpec((dm, 2 * dm), const2),
            pl.BlockSpec((1, HEAD_DIM), const2),
            pl.BlockSpec((tm, HEAD_DIM), lambda t: (t, 0)),
            pl.BlockSpec((tm, HEAD_DIM), lambda t: (t, 0)),
            pl.BlockSpec((HEAD_DIM, HEAD_DIM), const2),
            pl.BlockSpec((tm, tm), const2),
            pl.BlockSpec((tm, tm), const2),
        ],
        out_specs=[row, row] + [_residue_major_spec(tm, d, dm, tps) for d in dils for _ in range(2)],
        out_shape=[jax.ShapeDtypeStruct((n_tok, dm), BF16)] * 2
        + [_residue_major_shape(batch, seq, tm, d, dm, BF16) for d in dils for _ in range(2)],
        compiler_params=_cparams(("parallel",)),
        name="shared_kv",
    )(x, gain, w_kv, k_gain, cosf, sinf, rot_mat, *perms)
    return [(outs[0], outs[1]), (outs[2], outs[3]), (outs[4], outs[5])]


def _q_kernel(x_ref, gain_ref, w_ref, qgain_ref, cos_ref, sin_ref, rot_ref, perm1_ref, perm2_ref,
              q0_ref, q1_ref, q2_ref, qbuf_ref):
    dm = x_ref.shape[1]
    h = _rms(x_ref[...], gain_ref[...]).astype(BF16)
    q = _dot(h, w_ref[...])
    cosf, sinf, qg, rot = cos_ref[...], sin_ref[...], qgain_ref[...], rot_ref[...]
    for hd in range(q.shape[1] // HEAD_DIM):
        sl = slice(hd * HEAD_DIM, (hd + 1) * HEAD_DIM)
        y = _head_norm_rope(q[:, sl], qg, cosf, sinf, rot).astype(BF16)
        if hd < KV_HEADS:
            q0_ref[:, sl] = y
        else:
            qbuf_ref[:, hd * HEAD_DIM - dm:(hd + 1) * HEAD_DIM - dm] = y
    _store_residue_major(perm1_ref, qbuf_ref[:, :dm], q1_ref)
    _store_residue_major(perm2_ref, qbuf_ref[:, dm:], q2_ref)


def query_proj(x, gain, w_q, q_gain, cosf, sinf, rot_mat, perms, *, batch, seq, tm=512):
    n_tok, dm = x.shape
    nq = w_q.shape[1]
    tps = seq // tm
    dils = [d for _, d in DILATION_GROUPS[1:]]
    const2 = lambda t: (0, 0)
    return pl.pallas_call(
        _q_kernel,
        grid=(n_tok // tm,),
        in_specs=[
            pl.BlockSpec((tm, dm), lambda t: (t, 0)),
            pl.BlockSpec((1, dm), const2),
            pl.BlockSpec((dm, nq), const2),
            pl.BlockSpec((1, HEAD_DIM), const2),
            pl.BlockSpec((tm, HEAD_DIM), lambda t: (t, 0)),
            pl.BlockSpec((tm, HEAD_DIM), lambda t: (t, 0)),
            pl.BlockSpec((HEAD_DIM, HEAD_DIM), const2),
            pl.BlockSpec((tm, tm), const2),
            pl.BlockSpec((tm, tm), const2),
        ],
        out_specs=[pl.BlockSpec((tm, dm), lambda t: (t, 0))] + [_residue_major_spec(tm, d, dm, tps) for d in dils],
        out_shape=[jax.ShapeDtypeStruct((n_tok, dm), BF16)]
        + [_residue_major_shape(batch, seq, tm, d, dm, BF16) for d in dils],
        scratch_shapes=[pltpu.VMEM((tm, nq - dm), BF16)],
        compiler_params=_cparams(("parallel",)),
        name="query_proj",
    )(x, gain, w_q, q_gain, cosf, sinf, rot_mat, *perms)


ATTN_BLOCK = 128
ATTN_ROWS_PER_STEP = 1024
ATTN_MAX_RESIDUES_PER_STEP = 4
ATTN_BAND_GROUPS_PER_ITER = 2


def _attn_scores(q, k):
    return lax.dot_general(q, k, (((1,), (1,)), ((), ())), preferred_element_type=F32) * (HEAD_DIM ** -0.5)


def _attn_softmax(s, mask):
    s = jnp.where(mask, s, NEG_INF)
    m = jnp.max(s, axis=-1, keepdims=True)
    e = jnp.exp(s - m)
    den = jnp.sum(e, axis=-1, keepdims=True)
    return (e / den).astype(BF16), m + jnp.log(den)


def _attn_kernel(q_ref, k_ref, v_ref, o_ref, lse_ref):
    n_res, length = q_ref.shape[0], q_ref.shape[1]
    blk = ATTN_BLOCK
    n_blk = length // blk
    qi = lax.broadcasted_iota(jnp.int32, (blk, blk), 0)
    ki = lax.broadcasted_iota(jnp.int32, (blk, blk), 1)
    mask_first = ki <= qi
    qi2 = lax.broadcasted_iota(jnp.int32, (blk, 2 * blk), 0)
    ki2 = lax.broadcasted_iota(jnp.int32, (blk, 2 * blk), 1)
    dist = qi2 + blk - ki2
    mask_band = (dist >= 0) & (dist <= blk)
    lane = lax.broadcasted_iota(jnp.int32, (blk, LANES), 1)

    def query_blocks(blocks):
        heads = [slice(hd * HEAD_DIM, (hd + 1) * HEAD_DIM) for hd in range(KV_HEADS)]
        chains = [(q_rows, k_rows, mask, r) for q_rows, k_rows, mask in blocks for r in range(n_res)]
        scores = [[_attn_scores(q_ref[r, q_rows, sl], k_ref[r, k_rows, sl]) for sl in heads]
                  for q_rows, k_rows, _, r in chains]
        probs = [[_attn_softmax(s, mask) for s in row] for row, (_, _, mask, _) in zip(scores, chains)]
        for row, (q_rows, k_rows, _, r) in zip(probs, chains):
            lse_tile = jnp.zeros((blk, LANES), F32)
            for hd, sl in enumerate(heads):
                p, lse = row[hd]
                o_ref[r, q_rows, sl] = _dot(p, v_ref[r, k_rows, sl]).astype(o_ref.dtype)
                lse_tile = jnp.where(lane == hd, lse, lse_tile)
            lse_ref[r, q_rows, :] = lse_tile

    def band_block(n):
        q0, k0 = n * blk, (n - 1) * blk
        if not isinstance(n, int):
            q0, k0 = pl.multiple_of(q0, blk), pl.multiple_of(k0, blk)
        return pl.ds(q0, blk), pl.ds(k0, 2 * blk), mask_band

    first = (pl.ds(0, blk), pl.ds(0, blk), mask_first)
    per_iter = max(1, ATTN_BAND_GROUPS_PER_ITER // n_res)
    if n_blk == 1 or per_iter == 1:
        query_blocks([first])

        def body(n, carry):
            query_blocks([band_block(n)])
            return carry

        lax.fori_loop(1, n_blk, body, 0)
    else:
        assert per_iter == 2 and n_blk % 2 == 0
        query_blocks([first, band_block(1)])

        def body(i, carry):
            query_blocks([band_block(2 * i), band_block(2 * i + 1)])
            return carry

        lax.fori_loop(1, n_blk // 2, body, 0)


def dilated_attention(q, k, v, *, batch, seq, dilation):
    width = k.shape[-1]
    length = seq // dilation
    n_res = max(1, min(dilation, ATTN_MAX_RESIDUES_PER_STEP, ATTN_ROWS_PER_STEP // length))
    spec = pl.BlockSpec((None, n_res, length, width), lambda b, r: (b, r, 0, 0))
    return pl.pallas_call(
        _attn_kernel,
        grid=(batch, dilation // n_res),
        in_specs=[spec, spec, spec],
        out_specs=[spec, pl.BlockSpec((None, n_res, length, LANES), lambda b, r: (b, r, 0, 0))],
        out_shape=[
            jax.ShapeDtypeStruct((batch, dilation, length, width), BF16),
            jax.ShapeDtypeStruct((batch, dilation, length, LANES), F32),
        ],
        compiler_params=_cparams(("parallel", "parallel")),
        name=f"dilated_attention_d{dilation}",
    )(q, k, v)


def _split3_bf16(v):
    hi = v.astype(BF16)
    r1 = v - hi.astype(F32)
    mid = r1.astype(BF16)
    lo = (r1 - mid.astype(F32)).astype(BF16)
    return hi, mid, lo


def _attn_out_kernel(x_ref, o0_ref, o1_ref, o2_ref, l0_ref, l1_ref, l2_ref, unperm1_ref, unperm2_ref,
                     expand_ref, wo_ref, out_ref):
    tm, dm = x_ref.shape
    outs = [o0_ref[...].astype(F32)]
    lses = [l0_ref[...]]
    for o_ref, l_ref, unperm_ref in ((o1_ref, l1_ref, unperm1_ref), (o2_ref, l2_ref, unperm2_ref)):
        unperm = unperm_ref[...]
        outs.append(_dot(unperm, o_ref[...].reshape(tm, dm)))
        pieces = _split3_bf16(l_ref[...].reshape(tm, LANES))
        lses.append(_dot(unperm, pieces[0]) + _dot(unperm, pieces[1]) + _dot(unperm, pieces[2]))
    m = jnp.maximum(jnp.maximum(lses[0], lses[1]), lses[2])
    es = [jnp.exp(l - m) for l in lses]
    inv = 1.0 / (es[0] + es[1] + es[2])
    expand = expand_ref[...]
    mix = None
    for e, o in zip(es, outs):
        w = e * inv
        w_hi = w.astype(BF16)
        w_lo = (w - w_hi.astype(F32)).astype(BF16)
        w_full = _dot(w_hi, expand) + _dot(w_lo, expand)
        term = w_full * o
        mix = term if mix is None else mix + term
    out_ref[...] = x_ref[...] + _dot(mix.astype(BF16), wo_ref[...])


def attention_out(x, outs, lses, unperms, expand, w_o, *, batch, seq, tm=512):
    n_tok, dm = x.shape
    tps = seq // tm
    dils = [d for _, d in DILATION_GROUPS]
    row = lambda t: (t, 0)
    const = lambda t: (0, 0)
    o_specs = [pl.BlockSpec((tm, dm), row)] + [_residue_major_spec(tm, d, dm, tps) for d in dils[1:]]
    l_specs = [pl.BlockSpec((tm, LANES), row)] + [_residue_major_spec(tm, d, LANES, tps) for d in dils[1:]]
    o_args = [outs[0].reshape(n_tok, dm)] + [o.reshape(batch, d, tps, tm // d, dm) for o, d in zip(outs[1:], dils[1:])]
    l_args = [lses[0].reshape(n_tok, LANES)] + [l.reshape(batch, d, tps, tm // d, LANES)
                                                for l, d in zip(lses[1:], dils[1:])]
    return pl.pallas_call(
        _attn_out_kernel,
        grid=(n_tok // tm,),
        in_specs=[pl.BlockSpec((tm, dm), row)] + o_specs + l_specs
        + [pl.BlockSpec((tm, tm), const)] * 2 + [pl.BlockSpec((LANES, dm), const), pl.BlockSpec((dm, dm), const)],
        out_specs=pl.BlockSpec((tm, dm), row),
        out_shape=jax.ShapeDtypeStruct(x.shape, x.dtype),
        input_output_aliases={0: 0},
        compiler_params=_cparams(("parallel",)),
        name="attention_out",
    )(x, *o_args, *l_args, *unperms, expand, w_o)


def _pack_kernel(w_ref, o_ref, *, transpose):
    w = w_ref[...]
    if transpose:
        w = w.T
    o_ref[...] = pltpu.bitcast(w.astype(BF16), U32)


def pack_row_pairs(w_layers, layer, *, transpose=False, block=1024):
    _, rows, cols = w_layers.shape
    if transpose:
        out_shape = (cols // 2, rows)
        out_spec = pl.BlockSpec((cols // 2, block), lambda j: (0, j))
    else:
        out_shape = (rows // 2, cols)
        out_spec = pl.BlockSpec((block // 2, cols), lambda j: (j, 0))
    return pl.pallas_call(
        functools.partial(_pack_kernel, transpose=transpose),
        grid=(rows // block,),
        in_specs=[pl.BlockSpec((None, block, cols), lambda j: (layer, j, 0))],
        out_specs=out_spec,
        out_shape=jax.ShapeDtypeStruct(out_shape, U32),
        compiler_params=_cparams(("parallel",)),
        name="pack_weights_t" if transpose else "pack_weights",
    )(w_layers)


def _rope_tables(positions):
    half = ROPE_DIMS // 2
    inv_freq = 1.0 / (ROPE_THETA ** (jnp.arange(0, ROPE_DIMS, 2, dtype=F32) / ROPE_DIMS))
    ang = positions.astype(F32).reshape(-1, 1) * inv_freq
    cos, sin = jnp.cos(ang), jnp.sin(ang)
    pad = HEAD_DIM - ROPE_DIMS
    cosf = jnp.concatenate([cos, cos, jnp.ones((cos.shape[0], pad), F32)], axis=-1)
    sinf = jnp.concatenate([-sin, sin, jnp.zeros((sin.shape[0], pad), F32)], axis=-1)
    assert half * 2 == ROPE_DIMS
    return cosf, sinf


def _residue_perm(tm, dilation):
    m = tm // dilation
    idx = jnp.arange(tm)
    src_row = (idx % m) * dilation + idx // m
    return (src_row[:, None] == jnp.arange(tm)[None, :]).astype(BF16)


def kernel(x, p, positions, a_norm, a_w_in, a_conv, a_w_out, kv_norm, w_kv, k_norm, b_norm, b_w_q, q_norm, b_w_o, ffn_norm, peer_w_q, peer_sub_keys, peer_u, peer_v, ple_norm, ple_w_gate, ple_w_proj):
    batch, seq, dm = x.shape
    depth = p.shape[0]
    n_a = a_norm.shape[0]
    n_tok = batch * seq
    tm = 512
    xf = x.reshape(n_tok, dm)
    cosf, sinf = _rope_tables(positions)
    head_of_lane = jnp.arange(dm) // HEAD_DIM
    expand = (jnp.arange(LANES)[:, None] == head_of_lane[None, :]).astype(BF16)
    dils = [d for _, d in DILATION_GROUPS]
    assert dils[0] == 1 and all(w // d == ATTN_BLOCK for w, d in DILATION_GROUPS)
    perms = [_residue_perm(tm, d) for d in dils[1:]]
    half = ROPE_DIMS // 2
    lane = jnp.arange(HEAD_DIM)
    src_lane = jnp.where(lane < half, lane + half, lane - half)
    rot_mat = ((jnp.arange(HEAD_DIM)[:, None] == src_lane[None, :]) & (lane < ROPE_DIMS)[None, :]).astype(BF16)
    unperms = [pm.T for pm in perms]

    def residue_major(a, d):
        return a.reshape(batch, d, seq // d, a.shape[-1])

    p_flat = p.reshape(depth, n_tok, -1)
    kvs = None
    for i in range(depth):
        if i < n_a:
            xf = conv_mixer(xf, a_norm[i][None], a_w_in[i].astype(BF16), a_conv[i], a_w_out[i].astype(BF16), seq=seq,
                            in_place=i > 0)
        else:
            j = i - n_a
            qs = query_proj(xf, b_norm[j][None], b_w_q[j].astype(BF16), q_norm[j][None], cosf, sinf, rot_mat, perms,
                            batch=batch, seq=seq, tm=tm)
            outs, lses = [], []
            for q_g, (k_g, v_g), d in zip(qs, kvs, dils):
                o_g, lse_g = dilated_attention(residue_major(q_g, d), residue_major(k_g, d), residue_major(v_g, d),
                                               batch=batch, seq=seq, dilation=d)
                outs.append(o_g)
                lses.append(lse_g)
            xf = attention_out(xf, outs, lses, unperms, expand, b_w_o[j].astype(BF16), batch=batch, seq=seq, tm=tm)
        ht, kap, alp, rho, bet = peer_route(xf, ffn_norm[i][None], peer_w_q[i].T.astype(BF16),
                                            peer_sub_keys[i].astype(BF16))
        xf = peer_dense(xf, ht, pack_row_pairs(peer_u, i), pack_row_pairs(peer_v, i, transpose=True),
                        kap, alp, rho, bet, p_flat, i, ple_norm[i][None], ple_w_gate[i].astype(BF16),
                        ple_w_proj[i].astype(BF16))
        if i == n_a - 1:
            kvs = shared_kv(xf, kv_norm[None], w_kv.astype(BF16), k_norm[None], cosf, sinf, rot_mat, perms,
                            batch=batch, seq=seq, tm=tm)
    return xf.reshape(batch, seq, dm)
```
